```python
import jax, jax.numpy as jnp
from jax import lax
import numpy as np

D_MODEL = 2048
BATCH = 16
SEQ = 2048
DEPTH = 1
DEC_BATCH = 32
DEC_SEQ = 64
PAST_LEN = 2048

CHUNK = 64
MIX_WIDTH = D_MODEL
D_CONV = MIX_WIDTH // 2
CONV_GROUPS = 16
CONV_WIDTH = 3
D_ATTN = MIX_WIDTH - D_CONV
HEAD_DIM = 64
N_HEADS = D_ATTN // HEAD_DIM
N_KV_HEADS = 2
GQA_GROUP = N_HEADS // N_KV_HEADS
D_KV = N_KV_HEADS * HEAD_DIM
WINDOW = 128
WINDOW_CHUNKS = WINDOW // CHUNK
ROPE_THETA = 10000.0
D_FF = 5632
LN_EPS = 1e-5
RMS_EPS = 1e-6
ALPHA = (2.0 * DEPTH) ** 0.25
BETA = (8.0 * DEPTH) ** -0.25
ATTN_SCALE = HEAD_DIM ** -0.5
D_IN_MIX = 3 * D_CONV + D_ATTN + 2 * D_KV
SPLITS = [D_CONV, 2 * D_CONV, 3 * D_CONV, 3 * D_CONV + D_ATTN, 3 * D_CONV + D_ATTN + D_KV]

kernel_name = "hybrid_conv_swa_sink_streaming_step"


def _layer_norm(x, g, b):
    xf = x.astype(jnp.float32)
    mu = jnp.mean(xf, axis=-1, keepdims=True)
    var = jnp.mean(jnp.square(xf - mu), axis=-1, keepdims=True)
    return ((xf - mu) * lax.rsqrt(var + LN_EPS) * g.astype(jnp.float32) + b.astype(jnp.float32)).astype(x.dtype)


def _rms_norm(x, g):
    xf = x.astype(jnp.float32)
    inv = lax.rsqrt(jnp.mean(xf * xf, axis=-1, keepdims=True) + RMS_EPS)
    return (xf * inv * g.astype(jnp.float32)).astype(x.dtype)


def _half_ffn_block(x, w_in, w_out, g, b):
    gate, up = jnp.split(x @ w_in, 2, axis=-1)
    return _layer_norm(ALPHA * x + 0.5 * ((jax.nn.silu(gate) * up) @ w_out), g, b)


def _rope(x, pos):
    half = HEAD_DIM // 2
    inv = ROPE_THETA ** (-jnp.arange(half, dtype=jnp.float32) / half)
    ang = pos.astype(jnp.float32)[:, None] * inv[None, :]
    cos = jnp.cos(ang)[None, :, None, :]
    sin = jnp.sin(ang)[None, :, None, :]
    xf = x.astype(jnp.float32)
    x1, x2 = xf[..., :half], xf[..., half:]
    return jnp.concatenate([x1 * cos - x2 * sin, x2 * cos + x1 * sin], axis=-1).astype(x.dtype)


def _mix_project(x, w_mix_in, pos):
    B, T = x.shape[0], x.shape[1]
    b_gate, c_gate, hc, q, k, v = jnp.split(x @ w_mix_in, SPLITS, axis=-1)
    q = _rope(q.reshape(B, T, N_HEADS, HEAD_DIM), pos)
    k = _rope(k.reshape(B, T, N_KV_HEADS, HEAD_DIM), pos)
    v = v.reshape(B, T, N_KV_HEADS, HEAD_DIM)
    return b_gate, c_gate, hc, q, k, v


def _short_conv(b_gate, c_gate, hc, prev, conv_w):
    u = c_gate * hc
    T = u.shape[1]
    u_pad = jnp.concatenate([prev.astype(u.dtype), u], axis=1)
    z = conv_w[0] * u_pad[:, 0:T]
    for j in range(1, CONV_WIDTH):
        z = z + conv_w[j] * u_pad[:, j:j + T]
    return b_gate * z, u_pad[:, -(CONV_WIDTH - 1):]


def _sink_softmax(s, sink):
    m = jnp.maximum(jnp.max(s, axis=-1, keepdims=True), sink)
    e = jnp.exp(s - m)
    return e / (jnp.sum(e, axis=-1, keepdims=True) + jnp.exp(sink - m))


def _window_attn_prompt(q, k, v, sinks):
    B, S = q.shape[0], q.shape[1]
    nc = S // CHUNK
    band = (WINDOW_CHUNKS + 1) * CHUNK
    qb = q.reshape(B, nc, CHUNK, N_KV_HEADS, GQA_GROUP, HEAD_DIM)

    def bands(t):
        tp = jnp.pad(t, ((0, 0), (WINDOW, 0), (0, 0), (0, 0)))
        tc = tp.reshape(B, nc + WINDOW_CHUNKS, CHUNK, N_KV_HEADS, HEAD_DIM)
        return jnp.concatenate([tc[:, j:j + nc] for j in range(WINDOW_CHUNKS + 1)], axis=2)

    kb, vb = bands(k), bands(v)
    s = jnp.einsum('bnqhgd,bnkhd->bnhgqk', qb, kb, preferred_element_type=jnp.float32) * ATTN_SCALE
    key_pos = (jnp.arange(nc)[:, None] - WINDOW_CHUNKS) * CHUNK + jnp.arange(band)[None, :]
    valid = (key_pos >= 0)[None, :, None, None, None, :]
    s = jnp.where(valid, s, jnp.float32(-1e30))
    sink = sinks.astype(jnp.float32).reshape(1, 1, N_KV_HEADS, GQA_GROUP, 1, 1)
    p = _sink_softmax(s, sink).astype(v.dtype)
    o = jnp.einsum('bnhgqk,bnkhd->bnqhgd', p, vb)
    return o.reshape(B, S, D_ATTN)


def _window_attn_sample(q, k_all, v_all, sinks):
    B, T = q.shape[0], q.shape[1]
    qg = q.reshape(B, T, N_KV_HEADS, GQA_GROUP, HEAD_DIM)
    s = jnp.einsum('bqhgd,bkhd->bhgqk', qg, k_all, preferred_element_type=jnp.float32) * ATTN_SCALE
    sink = sinks.astype(jnp.float32).reshape(1, N_KV_HEADS, GQA_GROUP, 1, 1)
    p = _sink_softmax(s, sink).astype(v_all.dtype)
    o = jnp.einsum('bhgqk,bkhd->bqhgd', p, v_all)
    return o.reshape(B, T, D_ATTN)


def _mix_output(y_conv, y_attn, mix_norm_g, w_mix_out):
    y = jnp.concatenate([_rms_norm(y_conv, mix_norm_g[:D_CONV]), _rms_norm(y_attn, mix_norm_g[D_CONV:])], axis=-1)
    return y @ w_mix_out


def setup_inputs(seed: int = 0) -> dict:
    key = jax.random.key(seed)
    ks = jax.random.split(key, 24)
    f32 = jnp.float32
    nrm = lambda k, shape, scale: jax.random.normal(k, shape, f32) * scale
    return {
        "x_prompt": nrm(ks[0], (BATCH, SEQ, D_MODEL), 1.0),
        "x_sample": nrm(ks[1], (DEC_BATCH, DEC_SEQ, D_MODEL), 1.0),
        "cache_conv": nrm(ks[2], (DEPTH, DEC_BATCH, CONV_WIDTH - 1, D_CONV), 1.0),
        "cache_k": nrm(ks[3], (DEPTH, DEC_BATCH, WINDOW, N_KV_HEADS, HEAD_DIM), 1.0),
        "cache_v": nrm(ks[4], (DEPTH, DEC_BATCH, WINDOW, N_KV_HEADS, HEAD_DIM), 1.0),
        "ffn1_w_in": nrm(ks[5], (DEPTH, D_MODEL, 2 * D_FF), D_MODEL ** -0.5),
        "ffn1_w_out": nrm(ks[6], (DEPTH, D_FF, D_MODEL), BETA * D_FF ** -0.5),
        "ln1_g": 1.0 + nrm(ks[7], (DEPTH, D_MODEL), 0.05),
        "ln1_b": nrm(ks[8], (DEPTH, D_MODEL), 0.02),
        "w_mix_in": nrm(ks[9], (DEPTH, D_MODEL, D_IN_MIX), D_MODEL ** -0.5),
        "conv_w": nrm(ks[10], (DEPTH, CONV_WIDTH, D_CONV), CONV_WIDTH ** -0.5),
        "attn_sinks": nrm(ks[11], (DEPTH, N_HEADS), 1.0),
        "mix_norm_g": 1.0 + nrm(ks[12], (DEPTH, MIX_WIDTH), 0.05),
        "w_mix_out": nrm(ks[13], (DEPTH, MIX_WIDTH, D_MODEL), BETA * MIX_WIDTH ** -0.5),
        "ln2_g": 1.0 + nrm(ks[14], (DEPTH, D_MODEL), 0.05),
        "ln2_b": nrm(ks[15], (DEPTH, D_MODEL), 0.02),
        "ffn2_w_in": nrm(ks[16], (DEPTH, D_MODEL, 2 * D_FF), D_MODEL ** -0.5),
        "ffn2_w_out": nrm(ks[17], (DEPTH, D_FF, D_MODEL), BETA * D_FF ** -0.5),
        "ln3_g": 1.0 + nrm(ks[18], (DEPTH, D_MODEL), 0.05),
        "ln3_b": nrm(ks[19], (DEPTH, D_MODEL), 0.02),
    }


def reference(x_prompt, x_sample, cache_conv, cache_k, cache_v,
              ffn1_w_in, ffn1_w_out, ln1_g, ln1_b,
              w_mix_in, conv_w, attn_sinks, mix_norm_g, w_mix_out, ln2_g, ln2_b,
              ffn2_w_in, ffn2_w_out, ln3_g, ln3_b):
    xp, xs = x_prompt, x_sample
    Bp, Sp = xp.shape[0], xp.shape[1]
    Ts = xs.shape[1]
    pos_p = jnp.arange(Sp)
    pos_s = PAST_LEN + jnp.arange(Ts)
    conv_p, k_p, v_p, conv_s, k_s, v_s = [], [], [], [], [], []
    for l in range(DEPTH):
        xp = _half_ffn_block(xp, ffn1_w_in[l], ffn1_w_out[l], ln1_g[l], ln1_b[l])
        xs = _half_ffn_block(xs, ffn1_w_in[l], ffn1_w_out[l], ln1_g[l], ln1_b[l])

        bg, cg, hc, q, k, v = _mix_project(xp, w_mix_in[l], pos_p)
        zeros = jnp.zeros((Bp, CONV_WIDTH - 1, D_CONV), xp.dtype)
        yc, cst = _short_conv(bg, cg, hc, zeros, conv_w[l])
        ya = _window_attn_prompt(q, k, v, attn_sinks[l])
        xp = _layer_norm(ALPHA * xp + _mix_output(yc, ya, mix_norm_g[l], w_mix_out[l]), ln2_g[l], ln2_b[l])
        conv_p.append(cst)
        k_p.append(k[:, -WINDOW:])
        v_p.append(v[:, -WINDOW:])

        bg, cg, hc, q, k, v = _mix_project(xs, w_mix_in[l], pos_s)
        yc, cst = _short_conv(bg, cg, hc, cache_conv[l], conv_w[l])
        k_all = jnp.concatenate([cache_k[l].astype(k.dtype), k], axis=1)
        v_all = jnp.concatenate([cache_v[l].astype(v.dtype), v], axis=1)
        ya = _window_attn_sample(q, k_all, v_all, attn_sinks[l])
        xs = _layer_norm(ALPHA * xs + _mix_output(yc, ya, mix_norm_g[l], w_mix_out[l]), ln2_g[l], ln2_b[l])
        conv_s.append(cst)
        k_s.append(k_all[:, -WINDOW:])
        v_s.append(v_all[:, -WINDOW:])

        xp = _half_ffn_block(xp, ffn2_w_in[l], ffn2_w_out[l], ln3_g[l], ln3_b[l])
        xs = _half_ffn_block(xs, ffn2_w_in[l], ffn2_w_out[l], ln3_g[l], ln3_b[l])
    return (xp, xs, jnp.stack(conv_p), jnp.stack(k_p), jnp.stack(v_p), jnp.stack(conv_s), jnp.stack(k_s), jnp.stack(v_s))
```

```python
import functools

import jax
import jax.numpy as jnp
from jax import lax
from jax.experimental import pallas as pl
from jax.experimental.pallas import tpu as pltpu

CHUNK = 64
HEAD_DIM = 64
N_KV_HEADS = 2
WINDOW = 128
CONV_WIDTH = 3
PAST_LEN = 2048
ROPE_THETA = 10000.0
LN_EPS = 1e-5
RMS_EPS = 1e-6
ATTN_SCALE = HEAD_DIM ** -0.5
MASK_VALUE = -1e30

LANES = 128
SUBLANES = 8
V7X_VMEM_LIMIT_BYTES = 56 * 1024 * 1024

F32 = jnp.float32
BF16 = jnp.bfloat16


def _dot(a, b):
    return jnp.dot(a, b, preferred_element_type=F32)


def _layer_norm_rows(y, g, b):
    mu = jnp.mean(y, axis=-1, keepdims=True)
    d = y - mu
    var = jnp.mean(d * d, axis=-1, keepdims=True)
    return d * lax.rsqrt(var + LN_EPS) * g + b


def _rms_norm_rows(y, g):
    inv = lax.rsqrt(jnp.mean(y * y, axis=-1, keepdims=True) + RMS_EPS)
    return y * inv * g


def _ffn_kernel(x_ref, wg_ref, wu_ref, wo_ref, g_ref, b_ref, o_ref, xb_ref, *, alpha):
    j = pl.program_id(1)

    @pl.when(j == 0)
    def _():
        xb_ref[...] = x_ref[...].astype(BF16)
        o_ref[...] = jnp.zeros_like(o_ref)

    xb = xb_ref[...]
    gate = _dot(xb, wg_ref[...])
    up = _dot(xb, wu_ref[...])
    act = (gate / (1.0 + jnp.exp(-gate)) * up).astype(BF16)
    o_ref[...] += _dot(act, wo_ref[...])

    @pl.when(j == pl.num_programs(1) - 1)
    def _():
        y = alpha * x_ref[...] + 0.5 * o_ref[...]
        o_ref[...] = _layer_norm_rows(y, g_ref[...], b_ref[...])


def _ffn(x, w_in, w_out, g, b, *, alpha, tm, tf):
    m, d = x.shape
    f = w_out.shape[0]
    assert m % tm == 0 and f % tf == 0
    nf = f // tf
    return pl.pallas_call(
        functools.partial(_ffn_kernel, alpha=alpha),
        grid=(m // tm, nf),
        in_specs=[
            pl.BlockSpec((tm, d), lambda i, j: (i, 0)),
            pl.BlockSpec((d, tf), lambda i, j: (0, j)),
            pl.BlockSpec((d, tf), lambda i, j: (0, j + nf)),
            pl.BlockSpec((tf, d), lambda i, j: (j, 0)),
            pl.BlockSpec((1, d), lambda i, j: (0, 0)),
            pl.BlockSpec((1, d), lambda i, j: (0, 0)),
        ],
        out_specs=pl.BlockSpec((tm, d), lambda i, j: (i, 0)),
        out_shape=jax.ShapeDtypeStruct((m, d), F32),
        scratch_shapes=[pltpu.VMEM((tm, d), BF16)],
        compiler_params=pltpu.CompilerParams(
            dimension_semantics=("arbitrary", "arbitrary"),
            vmem_limit_bytes=V7X_VMEM_LIMIT_BYTES),
        name="ffn",
    )(x, w_in, w_in, w_out, g, b)


def _rotate_half(x):
    lane = lax.broadcasted_iota(jnp.int32, x.shape, 1)
    first_half = (lane % HEAD_DIM) < (HEAD_DIM // 2)
    return jnp.where(first_half, pltpu.roll(x, LANES - HEAD_DIM // 2, 1), pltpu.roll(x, HEAD_DIM // 2, 1))


def _mix_kernel(sink_ref, x_ref, w_ref, cw_ref, g_ref, cos_ref, sin_ref, ck_ref, cv_ref, cc_ref,
                y_ref, nconv_ref, nk_ref, nv_ref, kext, vext, uext,
                *, nb, tq, n_t, d_conv, d_attn, mask_halo):
    t = pl.program_id(1)
    d_kv = N_KV_HEADS * HEAD_DIM
    n_heads = d_attn // HEAD_DIM
    gqa = n_heads // N_KV_HEADS
    pad = SUBLANES
    rows = nb * tq

    @pl.when(t == 0)
    def _():
        kext[:, 0:WINDOW, :] = ck_ref[...]
        vext[:, 0:WINDOW, :] = cv_ref[...]
        uext[:, pad - (CONV_WIDTH - 1):pad, :] = cc_ref[...]

    xb = x_ref[...].reshape(rows, x_ref.shape[-1]).astype(BF16)

    b_gate = _dot(xb, w_ref[:, 0:d_conv])
    c_gate = _dot(xb, w_ref[:, d_conv:2 * d_conv])
    hc = _dot(xb, w_ref[:, 2 * d_conv:3 * d_conv])
    u = (c_gate * hc).reshape(nb, tq, d_conv)
    uext[:, pad:pad + tq, :] = u
    z = cw_ref[CONV_WIDTH - 1:CONV_WIDTH, :] * u
    for jj in range(CONV_WIDTH - 1):
        off = pad - (CONV_WIDTH - 1) + jj
        z = z + cw_ref[jj:jj + 1, :] * uext[:, off:off + tq, :]
    y_conv = b_gate.reshape(nb, tq, d_conv) * z
    y_ref[:, :, 0:d_conv] = _rms_norm_rows(y_conv, g_ref[:, 0:d_conv]).astype(BF16)

    @pl.when(t == n_t - 1)
    def _():
        nconv_ref[...] = uext[:, pad + tq - (CONV_WIDTH - 1):pad + tq, :]

    if n_t > 1:
        uext[:, pad - (CONV_WIDTH - 1):pad, :] = uext[:, pad + tq - (CONV_WIDTH - 1):pad + tq, :]

    cos = cos_ref[...]
    sin = sin_ref[...]
    q_off = 3 * d_conv
    kv = _dot(xb, w_ref[:, q_off + d_attn:q_off + d_attn + 2 * d_kv]).reshape(nb, tq, 2 * d_kv)
    for bi in range(nb):
        k_b = kv[bi, :, 0:d_kv]
        kext[bi, WINDOW:WINDOW + tq, :] = k_b * cos + _rotate_half(k_b) * sin
        vext[bi, WINDOW:WINDOW + tq, :] = kv[bi, :, d_kv:2 * d_kv]

    q = _dot(xb, w_ref[:, q_off:q_off + d_attn]).reshape(nb, tq, d_attn)
    cos_q = cos * ATTN_SCALE
    sin_q = sin * ATTN_SCALE

    lane = lax.broadcasted_iota(jnp.int32, (CHUNK, LANES), 1)
    low_half = lane < HEAD_DIM
    band = WINDOW + CHUNK
    lane_k = lax.broadcasted_iota(jnp.int32, (band, LANES), 1) < HEAD_DIM
    n_chunks = tq // CHUNK
    n_pairs = d_attn // LANES

    for bi in range(nb):
        q_pairs = []
        for p in range(n_pairs):
            qp = q[bi, :, p * LANES:(p + 1) * LANES]
            q_pairs.append(qp * cos_q + _rotate_half(qp) * sin_q)
        for c in range(n_chunks):
            r0 = c * CHUNK
            k_rows = kext[bi, r0:r0 + band, :]
            v_rows = vext[bi, r0:r0 + band, :]
            k_sw = pltpu.roll(k_rows, HEAD_DIM, 1)
            v_sw = pltpu.roll(v_rows, HEAD_DIM, 1)
            if mask_halo:
                key_idx = lax.broadcasted_iota(jnp.int32, (CHUNK, band), 1)
                key_pos = (t * n_chunks + (c - WINDOW // CHUNK)) * CHUNK + key_idx
                valid = key_pos >= 0
            y_blocks = [None] * n_pairs
            for h in range(N_KV_HEADS):
                k2 = jnp.where(lane_k, k_rows if h == 0 else k_sw, k_sw if h == 0 else k_rows).astype(BF16)
                v2 = jnp.where(lane_k, v_rows if h == 0 else v_sw, v_sw if h == 0 else v_rows).astype(BF16)
                q_stack = []
                for gi in range(gqa):
                    n = h * gqa + gi
                    qp = q_pairs[n // 2][r0:r0 + CHUNK, :]
                    keep = low_half if n % 2 == 0 else jnp.logical_not(low_half)
                    q_stack.append(jnp.where(keep, qp, 0.0).astype(BF16))
                qs = jnp.concatenate(q_stack, axis=0)
                s = lax.dot_general(qs, k2, (((1,), (1,)), ((), ())),
                                    preferred_element_type=F32)
                p_stack = []
                for gi in range(gqa):
                    n = h * gqa + gi
                    sg = s[gi * CHUNK:(gi + 1) * CHUNK, :]
                    if mask_halo:
                        sg = jnp.where(valid, sg, MASK_VALUE)
                    sink = sink_ref[n]
                    m_row = jnp.maximum(jnp.max(sg, axis=-1, keepdims=True), sink)
                    e = jnp.exp(sg - m_row)
                    denom = jnp.sum(e, axis=-1, keepdims=True) + jnp.exp(sink - m_row)
                    p_stack.append((e / denom).astype(BF16))
                ps = jnp.concatenate(p_stack, axis=0)
                o = _dot(ps, v2)
                for jp in range(gqa // 2):
                    o_even = o[(2 * jp) * CHUNK:(2 * jp + 1) * CHUNK, :]
                    o_odd = o[(2 * jp + 1) * CHUNK:(2 * jp + 2) * CHUNK, :]
                    y_blocks[(h * gqa) // 2 + jp] = jnp.where(low_half, o_even, o_odd)
            y_attn = jnp.concatenate(y_blocks, axis=1)
            y_ref[bi, r0:r0 + CHUNK, d_conv:d_conv + d_attn] = _rms_norm_rows(
                y_attn, g_ref[:, d_conv:d_conv + d_attn]).astype(BF16)

    @pl.when(t == n_t - 1)
    def _():
        nk_ref[...] = kext[:, tq:tq + WINDOW, :]
        nv_ref[...] = vext[:, tq:tq + WINDOW, :]

    if n_t > 1:
        kext[:, 0:WINDOW, :] = kext[:, tq:tq + WINDOW, :]
        vext[:, 0:WINDOW, :] = vext[:, tq:tq + WINDOW, :]


def _mix(x, w_in, conv_w, sinks, norm_g, cos, sin, cache_k, cache_v, cache_conv, *, nb, tq, mask_halo):
    bsz, t_len, d = x.shape
    d_conv = conv_w.shape[1]
    d_mix = norm_g.shape[1]
    d_attn = d_mix - d_conv
    d_kv = N_KV_HEADS * HEAD_DIM
    assert bsz % nb == 0 and t_len % tq == 0 and tq % CHUNK == 0
    assert d_kv == LANES and d_attn % LANES == 0
    assert w_in.shape[1] == 3 * d_conv + d_attn + 2 * d_kv
    n_t = t_len // tq
    assert n_t == 1 or tq >= WINDOW
    kernel = functools.partial(_mix_kernel, nb=nb, tq=tq, n_t=n_t, d_conv=d_conv, d_attn=d_attn,
                               mask_halo=mask_halo)
    const = lambda b, t: (0, 0)
    return pl.pallas_call(
        kernel,
        grid=(bsz // nb, n_t),
        in_specs=[
            pl.BlockSpec(memory_space=pltpu.SMEM),
            pl.BlockSpec((nb, tq, d), lambda b, t: (b, t, 0)),
            pl.BlockSpec(w_in.shape, const, pipeline_mode=pl.Buffered(1)),
            pl.BlockSpec(conv_w.shape, const),
            pl.BlockSpec(norm_g.shape, const),
            pl.BlockSpec((tq, LANES), lambda b, t: (t, 0)),
            pl.BlockSpec((tq, LANES), lambda b, t: (t, 0)),
            pl.BlockSpec((nb, WINDOW, d_kv), lambda b, t: (b, 0, 0)),
            pl.BlockSpec((nb, WINDOW, d_kv), lambda b, t: (b, 0, 0)),
            pl.BlockSpec((nb, CONV_WIDTH - 1, d_conv), lambda b, t: (b, 0, 0)),
        ],
        out_specs=[
            pl.BlockSpec((nb, tq, d_mix), lambda b, t: (b, t, 0)),
            pl.BlockSpec((nb, CONV_WIDTH - 1, d_conv), lambda b, t: (b, 0, 0)),
            pl.BlockSpec((nb, WINDOW, d_kv), lambda b, t: (b, 0, 0)),
            pl.BlockSpec((nb, WINDOW, d_kv), lambda b, t: (b, 0, 0)),
        ],
        out_shape=[
            jax.ShapeDtypeStruct((bsz, t_len, d_mix), BF16),
            jax.ShapeDtypeStruct((bsz, CONV_WIDTH - 1, d_conv), F32),
            jax.ShapeDtypeStruct((bsz, WINDOW, d_kv), F32),
            jax.ShapeDtypeStruct((bsz, WINDOW, d_kv), F32),
        ],
        scratch_shapes=[
            pltpu.VMEM((nb, WINDOW + tq, d_kv), F32),
            pltpu.VMEM((nb, WINDOW + tq, d_kv), F32),
            pltpu.VMEM((nb, SUBLANES + tq, d_conv), F32),
        ],
        compiler_params=pltpu.CompilerParams(
            dimension_semantics=("arbitrary", "arbitrary"),
            vmem_limit_bytes=V7X_VMEM_LIMIT_BYTES),
        name="mix",
    )(sinks, x, w_in, conv_w, norm_g, cos, sin, cache_k, cache_v, cache_conv)


def _mix_out_kernel(x_ref, y_ref, w_ref, g_ref, b_ref, o_ref, *, alpha):
    r = alpha * x_ref[...] + _dot(y_ref[...], w_ref[...])
    o_ref[...] = _layer_norm_rows(r, g_ref[...], b_ref[...])


def _mix_out(x, y, w, g, b, *, alpha, tm):
    m, d = x.shape
    assert m % tm == 0
    return pl.pallas_call(
        functools.partial(_mix_out_kernel, alpha=alpha),
        grid=(m // tm,),
        in_specs=[
            pl.BlockSpec((tm, d), lambda i: (i, 0)),
            pl.BlockSpec((tm, y.shape[1]), lambda i: (i, 0)),
            pl.BlockSpec(w.shape, lambda i: (0, 0)),
            pl.BlockSpec((1, d), lambda i: (0, 0)),
            pl.BlockSpec((1, d), lambda i: (0, 0)),
        ],
        out_specs=pl.BlockSpec((tm, d), lambda i: (i, 0)),
        out_shape=jax.ShapeDtypeStruct((m, d), F32),
        compiler_params=pltpu.CompilerParams(
            dimension_semantics=("arbitrary",),
            vmem_limit_bytes=V7X_VMEM_LIMIT_BYTES),
        name="mix_out",
    )(x, y, w, g, b)


def _rope_tables(pos):
    half = HEAD_DIM // 2
    inv = ROPE_THETA ** (-jnp.arange(half, dtype=F32) / half)
    ang = pos.astype(F32)[:, None] * inv[None, :]
    cos = jnp.cos(ang)
    sin = jnp.sin(ang)
    reps = LANES // HEAD_DIM
    return jnp.tile(jnp.concatenate([cos, cos], axis=1), (1, reps)), jnp.tile(jnp.concatenate([-sin, sin], axis=1), (1, reps))


def _row_tile(m, target):
    t = min(m, target)
    while m % t:
        t //= 2
    return t


def kernel(x_prompt, x_sample, cache_conv, cache_k, cache_v, ffn1_w_in, ffn1_w_out, ln1_g, ln1_b, w_mix_in, conv_w, attn_sinks, mix_norm_g, w_mix_out, ln2_g, ln2_b, ffn2_w_in, ffn2_w_out, ln3_g, ln3_b):
    depth = ffn1_w_in.shape[0]
    bp, sp, d = x_prompt.shape
    bs, ts, _ = x_sample.shape
    assert ts == CHUNK
    alpha = (2.0 * depth) ** 0.25
    d_conv = conv_w.shape[-1]
    d_kv = N_KV_HEADS * HEAD_DIM

    cos_p, sin_p = _rope_tables(jnp.arange(sp))
    cos_s, sin_s = _rope_tables(PAST_LEN + jnp.arange(ts))
    zeros_conv = jnp.zeros((bp, CONV_WIDTH - 1, d_conv), F32)
    zeros_kv = jnp.zeros((bp, WINDOW, d_kv), F32)

    xp = x_prompt.reshape(bp * sp, d)
    xs = x_sample.reshape(bs * ts, d)
    tm_p = _row_tile(bp * sp, 512)
    tm_s = _row_tile(bs * ts, 512)
    tf = 512
    outs = {k: [] for k in ("conv_p", "k_p", "v_p", "conv_s", "k_s", "v_s")}
    for l in range(depth):
        w1_in, w1_out = ffn1_w_in[l].astype(BF16), ffn1_w_out[l].astype(BF16)
        w2_in, w2_out = ffn2_w_in[l].astype(BF16), ffn2_w_out[l].astype(BF16)
        wm_in, wm_out = w_mix_in[l].astype(BF16), w_mix_out[l].astype(BF16)
        g1, b1 = ln1_g[l][None, :], ln1_b[l][None, :]
        g2, b2 = ln2_g[l][None, :], ln2_b[l][None, :]
        g3, b3 = ln3_g[l][None, :], ln3_b[l][None, :]
        gm = mix_norm_g[l][None, :]

        xp = _ffn(xp, w1_in, w1_out, g1, b1, alpha=alpha, tm=tm_p, tf=tf)
        xs = _ffn(xs, w1_in, w1_out, g1, b1, alpha=alpha, tm=tm_s, tf=tf)

        yp, conv_p, k_p, v_p = _mix(xp.reshape(bp, sp, d), wm_in, conv_w[l], attn_sinks[l], gm, cos_p, sin_p,
                                    zeros_kv, zeros_kv, zeros_conv, nb=1, tq=_row_tile(sp, 256), mask_halo=True)
        ys, conv_s, k_s, v_s = _mix(xs.reshape(bs, ts, d), wm_in, conv_w[l], attn_sinks[l], gm, cos_s, sin_s,
                                    cache_k[l].reshape(bs, WINDOW, d_kv), cache_v[l].reshape(bs, WINDOW, d_kv),
                                    cache_conv[l], nb=_row_tile(bs, 8), tq=ts, mask_halo=False)
        xp = _mix_out(xp, yp.reshape(bp * sp, -1), wm_out, g2, b2, alpha=alpha, tm=tm_p)
        xs = _mix_out(xs, ys.reshape(bs * ts, -1), wm_out, g2, b2, alpha=alpha, tm=tm_s)

        xp = _ffn(xp, w2_in, w2_out, g3, b3, alpha=alpha, tm=tm_p, tf=tf)
        xs = _ffn(xs, w2_in, w2_out, g3, b3, alpha=alpha, tm=tm_s, tf=tf)

        kv_shape_p = (bp, WINDOW, N_KV_HEADS, HEAD_DIM)
        kv_shape_s = (bs, WINDOW, N_KV_HEADS, HEAD_DIM)
        outs["conv_p"].append(conv_p)
        outs["k_p"].append(k_p.reshape(kv_shape_p))
        outs["v_p"].append(v_p.reshape(kv_shape_p))
        outs["conv_s"].append(conv_s)
        outs["k_s"].append(k_s.reshape(kv_shape_s))
        outs["v_s"].append(v_s.reshape(kv_shape_s))

    return (xp.reshape(bp, sp, d), xs.reshape(bs, ts, d),
            jnp.stack(outs["conv_p"]), jnp.stack(outs["k_p"]), jnp.stack(outs["v_p"]),
            jnp.stack(outs["conv_s"]), jnp.stack(outs["k_s"]), jnp.stack(outs["v_s"]))
```

```python
import functools

import jax
import jax.numpy as jnp
from jax import lax
from jax.experimental import pallas as pl
from jax.experimental.pallas import tpu as pltpu

CHUNK = 64
HEAD_DIM = 64
N_KV_HEADS = 2
WINDOW = 128
CONV_WIDTH = 3
PAST_LEN = 2048
ROPE_THETA = 10000.0
LN_EPS = 1e-5
RMS_EPS = 1e-6
ATTN_SCALE = HEAD_DIM ** -0.5
MASK_VALUE = -1e30

LANES = 128
SUBLANES = 8
V7X_VMEM_BYTES = 64 * 1024 * 1024
V7X_VMEM_REQUEST_CAP = V7X_VMEM_BYTES - 6 * 1024 * 1024

F32 = jnp.float32
BF16 = jnp.bfloat16
F32_BYTES = 4
BF16_BYTES = 2


def _dot(a, b):
    return jnp.dot(a, b, preferred_element_type=F32)


def _layer_norm_rows(y, g, b):
    mu = jnp.mean(y, axis=-1, keepdims=True)
    d = y - mu
    var = jnp.mean(d * d, axis=-1, keepdims=True)
    return d * lax.rsqrt(var + LN_EPS) * g + b


def _rms_norm_rows(y, g):
    inv = lax.rsqrt(jnp.mean(y * y, axis=-1, keepdims=True) + RMS_EPS)
    return y * inv * g


def _vmem_limit(n_bytes):
    assert n_bytes <= V7X_VMEM_REQUEST_CAP, n_bytes
    return int(n_bytes)


def _ffn_kernel(x_hbm, wg_ref, wu_ref, wo_ref, g_ref, b_ref, o_ref, xf_ref, xb_ref, act_ref, x_sem,
                *, alpha, nf, tm):
    i = pl.program_id(0)
    j = pl.program_id(1)

    def x_copy(tile):
        return pltpu.make_async_copy(x_hbm.at[pl.ds(tile * tm, tm), :], xf_ref, x_sem)

    def gate_up(slot):
        xb = xb_ref[...]
        gate = _dot(xb, wg_ref[...])
        up = _dot(xb, wu_ref[...])
        act_ref[slot] = (gate / (1.0 + jnp.exp(-gate)) * up).astype(BF16)

    def down(slot, rows=slice(None)):
        o_ref[rows, :] += _dot(act_ref[slot, rows, :], wo_ref[...])

    @pl.when(jnp.logical_and(i == 0, j == 0))
    def _():
        x_copy(0).start()

    @pl.when(j == 0)
    def _():
        x_copy(i).wait()
        x = xf_ref[...]
        xb_ref[...] = x.astype(BF16)
        o_ref[...] = (2.0 * alpha) * x
        gate_up(0)

    @pl.when(jnp.logical_and(j == 1, i + 1 < pl.num_programs(0)))
    def _():
        x_copy(i + 1).start()

    for parity in range(2):
        @pl.when(jnp.logical_and(jnp.logical_and(j > 0, j < nf), j % 2 == parity))
        def _(parity=parity):
            gate_up(parity)
            down(1 - parity)

    @pl.when(j == nf)
    def _():
        for r in range(2):
            rows = slice(r * (tm // 2), (r + 1) * (tm // 2))
            down((nf - 1) % 2, rows)
            o_ref[rows, :] = _layer_norm_rows(0.5 * o_ref[rows, :], g_ref[...], b_ref[...])


def _ffn(x, w_in, w_out, g, b, *, alpha, tm, tf):
    m, d = x.shape
    f = w_out.shape[0]
    assert m % tm == 0 and f % tf == 0 and tm % (2 * SUBLANES) == 0
    nf = f // tf
    vmem = (tm * d * F32_BYTES
            + 2 * tm * d * F32_BYTES
            + tm * d * BF16_BYTES
            + 2 * tm * tf * BF16_BYTES
            + 2 * 3 * d * tf * BF16_BYTES
            + 2 * tm * tf * F32_BYTES
            + tm * d * F32_BYTES)
    return pl.pallas_call(
        functools.partial(_ffn_kernel, alpha=alpha, nf=nf, tm=tm),
        grid=(m // tm, nf + 1),
        in_specs=[
            pl.BlockSpec(memory_space=pl.ANY),
            pl.BlockSpec((d, tf), lambda i, j: (0, jnp.minimum(j, nf - 1))),
            pl.BlockSpec((d, tf), lambda i, j: (0, jnp.minimum(j, nf - 1) + nf)),
            pl.BlockSpec((tf, d), lambda i, j: (jnp.maximum(j - 1, 0), 0)),
            pl.BlockSpec((1, d), lambda i, j: (0, 0)),
            pl.BlockSpec((1, d), lambda i, j: (0, 0)),
        ],
        out_specs=pl.BlockSpec((tm, d), lambda i, j: (i, 0)),
        out_shape=jax.ShapeDtypeStruct((m, d), F32),
        scratch_shapes=[pltpu.VMEM((tm, d), F32), pltpu.VMEM((tm, d), BF16), pltpu.VMEM((2, tm, tf), BF16),
                        pltpu.SemaphoreType.DMA],
        compiler_params=pltpu.CompilerParams(
            dimension_semantics=("arbitrary", "arbitrary"),
            vmem_limit_bytes=_vmem_limit(vmem)),
        name="ffn",
    )(x, w_in, w_in, w_out, g, b)


def _rotate_half(x):
    lane = lax.broadcasted_iota(jnp.int32, x.shape, 1)
    first_half = (lane % HEAD_DIM) < (HEAD_DIM // 2)
    return jnp.where(first_half, pltpu.roll(x, LANES - HEAD_DIM // 2, 1), pltpu.roll(x, HEAD_DIM // 2, 1))


def _mix_kernel(sink_ref, x_ref, w_ref, wo_ref, cw_ref, gm_ref, g_ref, b_ref, cos_ref, sin_ref,
                ck_ref, cv_ref, cc_ref,
                o_ref, nconv_ref, nk_ref, nv_ref,
                kext, vext, k2, v2, uext, qs, resid, ya, yc,
                *, nb, tq, n_t, n_tiles, d_conv, d_attn, mask_halo, alpha):
    s = pl.program_id(0)
    d_kv = N_KV_HEADS * HEAD_DIM
    gqa = d_attn // HEAD_DIM // N_KV_HEADS
    pad = SUBLANES
    tail = CONV_WIDTH - 1
    rows = nb * tq
    band = WINDOW + CHUNK
    ext = WINDOW + tq
    n_chunks = tq // CHUNK
    n_pairs = d_attn // LANES
    q_off = 3 * d_conv

    def stage_a(slot, t):
        first = t == 0
        x = x_ref[...].reshape(rows, x_ref.shape[-1])
        xb = x.astype(BF16)
        half = d_conv // 2
        col = lambda g, hf: slice(g * d_conv + hf * half, g * d_conv + (hf + 1) * half)

        b_gate = []
        for hf in range(2):
            b_gate.append(_dot(xb, w_ref[:, col(0, hf)]))
            yield
        c_gate = []
        for hf in range(2):
            c_gate.append(_dot(xb, w_ref[:, col(1, hf)]))
            yield
        y_half = []
        for hf in range(2):
            ch = slice(hf * half, (hf + 1) * half)
            hc = _dot(xb, w_ref[:, col(2, hf)])
            u = (c_gate[hf] * hc).reshape(nb, tq, half)
            uext[:, pad - tail:pad, ch] = jnp.where(first, cc_ref[:, :, ch], uext[:, pad + tq - tail:pad + tq, ch])
            uext[:, pad:pad + tq, ch] = u
            z = cw_ref[CONV_WIDTH - 1:CONV_WIDTH, ch] * u
            for jj in range(tail):
                off = pad - tail + jj
                z = z + cw_ref[jj:jj + 1, ch] * uext[:, off:off + tq, ch]
            y_half.append(b_gate[hf].reshape(nb, tq, half) * z)
            if hf == 0:
                yield
        nconv_ref[...] = uext[:, pad + tq - tail:pad + tq, :]
        ssq = sum(jnp.sum(y * y, axis=-1, keepdims=True) for y in y_half)
        inv = lax.rsqrt(ssq * (1.0 / d_conv) + RMS_EPS)
        for hf in range(2):
            ch = slice(hf * half, (hf + 1) * half)
            yc[:, ch] = (y_half[hf] * inv * gm_ref[:, ch]).reshape(rows, half).astype(BF16)
        yield

        cos = cos_ref[...]
        sin = sin_ref[...]
        kext[slot, :, 0:WINDOW, :] = jnp.where(first, ck_ref[...], kext[1 - slot, :, tq:ext, :])
        vext[slot, :, 0:WINDOW, :] = jnp.where(first, cv_ref[...], vext[1 - slot, :, tq:ext, :])
        kv = _dot(xb, w_ref[:, q_off + d_attn:q_off + d_attn + 2 * d_kv]).reshape(nb, tq, 2 * d_kv)
        for bi in range(nb):
            k_b = kv[bi, :, 0:d_kv]
            kext[slot, bi, WINDOW:ext, :] = k_b * cos + _rotate_half(k_b) * sin
            vext[slot, bi, WINDOW:ext, :] = kv[bi, :, d_kv:2 * d_kv]
        nk_ref[...] = kext[slot, :, tq:ext, :]
        nv_ref[...] = vext[slot, :, tq:ext, :]

        low_half_ext = lax.broadcasted_iota(jnp.int32, (ext, LANES), 1) < HEAD_DIM
        for bi in range(nb):
            for src, dst in ((kext, k2), (vext, v2)):
                rows_f = src[slot, bi]
                rows_sw = pltpu.roll(rows_f, HEAD_DIM, 1)
                dst[slot, 0, bi] = jnp.where(low_half_ext, rows_f, rows_sw).astype(BF16)
                dst[slot, 1, bi] = jnp.where(low_half_ext, rows_sw, rows_f).astype(BF16)
        yield

        cos_q = cos * ATTN_SCALE
        sin_q = sin * ATTN_SCALE
        for hf in range(2):
            lo = hf * (d_attn // 2)
            q = _dot(xb, w_ref[:, q_off + lo:q_off + lo + d_attn // 2]).reshape(nb, tq, d_attn // 2)
            for bi in range(nb):
                for p in range(n_pairs // 2):
                    qp = q[bi, :, p * LANES:(p + 1) * LANES]
                    qs[slot, bi, :, lo + p * LANES:lo + (p + 1) * LANES] = (
                        qp * cos_q + _rotate_half(qp) * sin_q).astype(BF16)
            yield

        resid[slot] = alpha * x + _dot(yc[...], wo_ref[0:d_conv, :])

    def stage_b(slot, t):
        low_half =lax.broadcasted_iota(jnp.int32, (CHUNK, LANES), 1) < HEAD_DIM
        low_half_row = lax.broadcasted_iota(jnp.int32, (1, LANES), 1) < HEAD_DIM
        zero_q = jnp.zeros((CHUNK, LANES), BF16)
        for bi in range(nb):
            for c in range(n_chunks):
                r0 = c * CHUNK
                masked = mask_halo and c < WINDOW // CHUNK
                if masked:
                    key_pos = (t * n_chunks + (c - WINDOW // CHUNK)) * CHUNK + lax.broadcasted_iota(
                        jnp.int32, (band, 1), 0)
                    valid = key_pos >= 0
                y_blocks = [None] * n_pairs
                for h in range(N_KV_HEADS):
                    q_stack = []
                    sink_cols = []
                    for gi in range(gqa):
                        n = h * gqa + gi
                        qp = qs[slot, bi, r0:r0 + CHUNK, (n // 2) * LANES:(n // 2 + 1) * LANES]
                        keep = low_half if n % 2 == 0 else jnp.logical_not(low_half)
                        q_stack.append(jnp.where(keep, qp, zero_q))
                        if gi % 2 == 0:
                            sink_cols.append(jnp.where(low_half_row, sink_ref[n], sink_ref[n + 1]))
                    qm = jnp.concatenate(q_stack, axis=0)
                    sink = jnp.concatenate(sink_cols, axis=1)
                    sc = lax.dot_general(k2[slot, h, bi, r0:r0 + band, :], qm, (((1,), (1,)), ((), ())),
                                         preferred_element_type=F32)
                    if masked:
                        sc = jnp.where(valid, sc, MASK_VALUE)
                    m_col = jnp.maximum(jnp.max(sc, axis=0, keepdims=True), sink)
                    e = jnp.exp(sc - m_col)
                    inv_denom = 1.0 / (jnp.sum(e, axis=0, keepdims=True) + jnp.exp(sink - m_col))
                    e = e.astype(BF16)
                    yield
                    o_t = lax.dot_general(v2[slot, h, bi, r0:r0 + band, :], e,
                                          (((0,), (0,)), ((), ())), preferred_element_type=F32)
                    o = (o_t * inv_denom).T
                    for jp in range(gqa // 2):
                        o_even = o[(2 * jp) * CHUNK:(2 * jp + 1) * CHUNK, :]
                        o_odd = o[(2 * jp + 1) * CHUNK:(2 * jp + 2) * CHUNK, :]
                        y_blocks[(h * gqa) // 2 + jp] = jnp.where(low_half, o_even, o_odd)
                y_attn = jnp.concatenate(y_blocks, axis=1)
                ya[bi, r0:r0 + CHUNK, :] = _rms_norm_rows(
                    y_attn, gm_ref[:, d_conv:d_conv + d_attn]).astype(BF16)
        r = resid[slot] + _dot(ya[...].reshape(rows, d_attn), wo_ref[d_conv:d_conv + d_attn, :])
        o_ref[...] = _layer_norm_rows(r, g_ref[...], b_ref[...]).reshape(o_ref.shape)

    def run(*stages):
        live = list(stages)
        while live:
            for g in list(live):
                if next(g, StopIteration) is StopIteration:
                    live.remove(g)

    @pl.when(s == 0)
    def _():
        run(stage_a(0, 0))

    for parity in range(2):
        @pl.when(jnp.logical_and(jnp.logical_and(s > 0, s < n_tiles), s % 2 == parity))
        def _(parity=parity):
            run(stage_b(1 - parity, (s - 1) % n_t), stage_a(parity, s % n_t))

    @pl.when(s == n_tiles)
    def _():
        run(stage_b((n_tiles - 1) % 2, (n_tiles - 1) % n_t))


def _mix(x, w_in, w_out, conv_w, sinks, norm_g, ln_g, ln_b, cos, sin, cache_k, cache_v, cache_conv,
         *, nb, tq, mask_halo, alpha):
    bsz, t_len, d = x.shape
    d_conv = conv_w.shape[1]
    d_mix = norm_g.shape[1]
    d_attn = d_mix - d_conv
    d_kv = N_KV_HEADS * HEAD_DIM
    assert bsz % nb == 0 and t_len % tq == 0 and tq % CHUNK == 0
    assert d_kv == LANES and d_attn % LANES == 0 and (d_attn // HEAD_DIM) % (2 * N_KV_HEADS) == 0
    assert w_in.shape[1] == 3 * d_conv + d_attn + 2 * d_kv
    n_t = t_len // tq
    assert n_t == 1 or tq >= WINDOW
    n_tiles = (bsz // nb) * n_t
    rows = nb * tq
    ext = WINDOW + tq
    vmem = (w_in.size * BF16_BYTES + w_out.size * BF16_BYTES
            + 2 * 2 * rows * d * F32_BYTES
            + 2 * 2 * tq * LANES * F32_BYTES
            + 2 * 2 * nb * ext * d_kv * F32_BYTES
            + 2 * 2 * N_KV_HEADS * nb * ext * LANES * BF16_BYTES
            + nb * (SUBLANES + tq) * d_conv * F32_BYTES
            + 3 * rows * d_attn * BF16_BYTES
            + rows * d_conv * BF16_BYTES
            + 2 * rows * d * F32_BYTES
            + 4 * 2 * nb * WINDOW * d_kv * F32_BYTES
            + 4 * rows * d_conv * F32_BYTES
            + 2 * rows * d * F32_BYTES)
    kernel = functools.partial(_mix_kernel, nb=nb, tq=tq, n_t=n_t, n_tiles=n_tiles, d_conv=d_conv,
                               d_attn=d_attn, mask_halo=mask_halo, alpha=alpha)
    tile_a = lambda s: jnp.minimum(s, n_tiles - 1)
    tile_b = lambda s: jnp.maximum(s - 1, 0)
    const = lambda s: (0, 0)
    seq_a = lambda s: (tile_a(s) // n_t, 0, 0)
    return pl.pallas_call(
        kernel,
        grid=(n_tiles + 1,),
        in_specs=[
            pl.BlockSpec(memory_space=pltpu.SMEM),
            pl.BlockSpec((nb, tq, d), lambda s: (tile_a(s) // n_t, tile_a(s) % n_t, 0)),
            pl.BlockSpec(w_in.shape, const, pipeline_mode=pl.Buffered(1)),
            pl.BlockSpec(w_out.shape, const, pipeline_mode=pl.Buffered(1)),
            pl.BlockSpec(conv_w.shape, const),
            pl.BlockSpec(norm_g.shape, const),
            pl.BlockSpec(ln_g.shape, const),
            pl.BlockSpec(ln_b.shape, const),
            pl.BlockSpec((tq, LANES), lambda s: (tile_a(s) % n_t, 0)),
            pl.BlockSpec((tq, LANES), lambda s: (tile_a(s) % n_t, 0)),
            pl.BlockSpec((nb, WINDOW, d_kv), seq_a),
            pl.BlockSpec((nb, WINDOW, d_kv), seq_a),
            pl.BlockSpec((nb, CONV_WIDTH - 1, d_conv), seq_a),
        ],
        out_specs=[
            pl.BlockSpec((nb, tq, d), lambda s: (tile_b(s) // n_t, tile_b(s) % n_t, 0)),
            pl.BlockSpec((nb, CONV_WIDTH - 1, d_conv), seq_a),
            pl.BlockSpec((nb, WINDOW, d_kv), seq_a),
            pl.BlockSpec((nb, WINDOW, d_kv), seq_a),
        ],
        out_shape=[
            jax.ShapeDtypeStruct((bsz, t_len, d), F32),
            jax.ShapeDtypeStruct((bsz, CONV_WIDTH - 1, d_conv), F32),
            jax.ShapeDtypeStruct((bsz, WINDOW, d_kv), F32),
            jax.ShapeDtypeStruct((bsz, WINDOW, d_kv), F32),
        ],
        scratch_shapes=[
            pltpu.VMEM((2, nb, ext, d_kv), F32),
            pltpu.VMEM((2, nb, ext, d_kv), F32),
            pltpu.VMEM((2, N_KV_HEADS, nb, ext, LANES), BF16),
            pltpu.VMEM((2, N_KV_HEADS, nb, ext, LANES), BF16),
            pltpu.VMEM((nb, SUBLANES + tq, d_conv), F32),
            pltpu.VMEM((2, nb, tq, d_attn), BF16),
            pltpu.VMEM((2, rows, d), F32),
            pltpu.VMEM((nb, tq, d_attn), BF16),
            pltpu.VMEM((rows, d_conv), BF16),
        ],
        compiler_params=pltpu.CompilerParams(
            dimension_semantics=("arbitrary",),
            vmem_limit_bytes=_vmem_limit(vmem)),
        name="mix",
    )(sinks, x, w_in, w_out, conv_w, norm_g, ln_g, ln_b, cos, sin, cache_k, cache_v, cache_conv)


def _rope_tables(pos):
    half = HEAD_DIM // 2
    inv = ROPE_THETA ** (-jnp.arange(half, dtype=F32) / half)
    ang = pos.astype(F32)[:, None] * inv[None, :]
    cos = jnp.cos(ang)
    sin = jnp.sin(ang)
    reps = LANES // HEAD_DIM
    return jnp.tile(jnp.concatenate([cos, cos], axis=1), (1, reps)), jnp.tile(jnp.concatenate([-sin, sin], axis=1), (1, reps))


def _largest_tile(n, target):
    t = min(n, target)
    while n % t:
        t //= 2
    return t


def _plan(bp, sp, bs, ts):
    ffn_rows = 1024
    ffn_chunk = 512
    mix_rows = 256
    return dict(
        tm_p=_largest_tile(bp * sp, ffn_rows), tm_s=_largest_tile(bs * ts, ffn_rows), tf=ffn_chunk,
        mix_p=dict(nb=1, tq=_largest_tile(sp, mix_rows)),
        mix_s=dict(nb=_largest_tile(bs, mix_rows // ts), tq=ts))


def kernel(x_prompt, x_sample, cache_conv, cache_k, cache_v, ffn1_w_in, ffn1_w_out, ln1_g, ln1_b, w_mix_in, conv_w, attn_sinks, mix_norm_g, w_mix_out, ln2_g, ln2_b, ffn2_w_in, ffn2_w_out, ln3_g, ln3_b):
    depth = ffn1_w_in.shape[0]
    bp, sp, d = x_prompt.shape
    bs, ts, _ = x_sample.shape
    assert ts == CHUNK
    alpha = (2.0 * depth) ** 0.25
    d_conv = conv_w.shape[-1]
    d_kv = N_KV_HEADS * HEAD_DIM
    plan = _plan(bp, sp, bs, ts)
    tf = _largest_tile(ffn1_w_out.shape[1], plan["tf"])

    cos_p, sin_p = _rope_tables(jnp.arange(sp))
    cos_s, sin_s = _rope_tables(PAST_LEN + jnp.arange(ts))
    zeros_conv = jnp.zeros((bp, CONV_WIDTH - 1, d_conv), F32)
    zeros_kv = jnp.zeros((bp, WINDOW, d_kv), F32)

    xp = x_prompt.reshape(bp * sp, d)
    xs = x_sample.reshape(bs * ts, d)
    outs = {k: [] for k in ("conv_p", "k_p", "v_p", "conv_s", "k_s", "v_s")}
    for l in range(depth):
        w1_in, w1_out = ffn1_w_in[l].astype(BF16), ffn1_w_out[l].astype(BF16)
        w2_in, w2_out = ffn2_w_in[l].astype(BF16), ffn2_w_out[l].astype(BF16)
        wm_in, wm_out = w_mix_in[l].astype(BF16), w_mix_out[l].astype(BF16)
        g1, b1 = ln1_g[l][None, :], ln1_b[l][None, :]
        g2, b2 = ln2_g[l][None, :], ln2_b[l][None, :]
        g3, b3 = ln3_g[l][None, :], ln3_b[l][None, :]
        gm = mix_norm_g[l][None, :]

        xp = _ffn(xp, w1_in, w1_out, g1, b1, alpha=alpha, tm=plan["tm_p"], tf=tf)
        xs = _ffn(xs, w1_in, w1_out, g1, b1, alpha=alpha, tm=plan["tm_s"], tf=tf)

        xp, conv_p, k_p, v_p = _mix(
            xp.reshape(bp, sp, d), wm_in, wm_out, conv_w[l], attn_sinks[l], gm, g2, b2, cos_p, sin_p,
            zeros_kv, zeros_kv, zeros_conv, mask_halo=True, alpha=alpha, **plan["mix_p"])
        xs, conv_s, k_s, v_s = _mix(
            xs.reshape(bs, ts, d), wm_in, wm_out, conv_w[l], attn_sinks[l], gm, g2, b2, cos_s, sin_s,
            cache_k[l].reshape(bs, WINDOW, d_kv), cache_v[l].reshape(bs, WINDOW, d_kv), cache_conv[l],
            mask_halo=False, alpha=alpha, **plan["mix_s"])
        xp = xp.reshape(bp * sp, d)
        xs = xs.reshape(bs * ts, d)

        xp = _ffn(xp, w2_in, w2_out, g3, b3, alpha=alpha, tm=plan["tm_p"], tf=tf)
        xs = _ffn(xs, w2_in, w2_out, g3, b3, alpha=alpha, tm=plan["tm_s"], tf=tf)

        kv_shape_p = (bp, WINDOW, N_KV_HEADS, HEAD_DIM)
        kv_shape_s = (bs, WINDOW, N_KV_HEADS, HEAD_DIM)
        outs["conv_p"].append(conv_p)
        outs["k_p"].append(k_p.reshape(kv_shape_p))
        outs["v_p"].append(v_p.reshape(kv_shape_p))
        outs["conv_s"].append(conv_s)
        outs["k_s"].append(k_s.reshape(kv_shape_s))
        outs["v_s"].append(v_s.reshape(kv_shape_s))

    return (xp.reshape(bp, sp, d), xs.reshape(bs, ts, d),
            jnp.stack(outs["conv_p"]), jnp.stack(outs["k_p"]), jnp.stack(outs["v_p"]),
            jnp.stack(outs["conv_s"]), jnp.stack(outs["k_s"]), jnp.stack(outs["v_s"]))
```

```python
import functools

import jax
import jax.numpy as jnp
from jax import lax
from jax.experimental import pallas as pl
from jax.experimental.pallas import tpu as pltpu

CHUNK = 64
HEAD_DIM = 64
N_KV_HEADS = 2
WINDOW = 128
CONV_WIDTH = 3
PAST_LEN = 2048
ROPE_THETA = 10000.0
LN_EPS = 1e-5
RMS_EPS = 1e-6
ATTN_SCALE = HEAD_DIM ** -0.5
MASK_VALUE = -1e30

LANES = 128
SUBLANES = 8
V7X_VMEM_BYTES = 64 * 1024 * 1024
V7X_VMEM_REQUEST_CAP = V7X_VMEM_BYTES - 6 * 1024 * 1024

F32 = jnp.float32
BF16 = jnp.bfloat16
F32_BYTES = 4
BF16_BYTES = 2
BF16_ROW_TILE = 2 * SUBLANES


def _dot(a, b):
    return jnp.dot(a, b, preferred_element_type=F32)


def _layer_norm_rows(y, g, b):
    mu = jnp.mean(y, axis=-1, keepdims=True)
    d = y - mu
    var = jnp.mean(d * d, axis=-1, keepdims=True)
    return d * lax.rsqrt(var + LN_EPS) * g + b


def _rms_norm_rows(y, g):
    inv = lax.rsqrt(jnp.mean(y * y, axis=-1, keepdims=True) + RMS_EPS)
    return y * inv * g


def _vmem_limit(n_bytes):
    assert n_bytes <= V7X_VMEM_REQUEST_CAP, n_bytes
    return int(n_bytes)


def _convert_heights(arrays, n_steps):
    heights = []
    for a in arrays:
        r = a.shape[0]
        heights.append(next(h for h in range(BF16_ROW_TILE, r + 1, BF16_ROW_TILE)
                            if r % h == 0 and r // h <= n_steps))
    return heights


def _convert_specs(arrays, heights, step_of):
    specs, shapes, vmem = [], [], 0
    for a, h in zip(arrays, heights):
        last = a.shape[0] // h - 1
        specs.append(pl.BlockSpec((h, a.shape[1]), lambda *idx, last=last: (jnp.minimum(step_of(*idx), last), 0)))
        shapes.append(jax.ShapeDtypeStruct(a.shape, BF16))
        vmem += 2 * h * a.shape[1] * (F32_BYTES + BF16_BYTES)
    return specs, shapes, vmem


def _convert_step(step, src_refs, dst_refs, arrays_rows):
    for src, dst, rows in zip(src_refs, dst_refs, arrays_rows):
        @pl.when(step < rows // src.shape[0])
        def _(src=src, dst=dst):
            dst[...] = src[...].astype(BF16)


def _ffn_kernel(x_hbm, wg_ref, wu_ref, wo_ref, g_ref, b_ref, *rest, alpha, nf, tm, cv_rows):
    n_cv = len(cv_rows)
    cv_src, o_ref, cv_dst = rest[:n_cv], rest[n_cv], rest[n_cv + 1:2 * n_cv + 1]
    xf_ref, xb_ref, act_ref, x_sem = rest[2 * n_cv + 1:]
    i = pl.program_id(0)
    j = pl.program_id(1)
    _convert_step(i * (nf + 1) + j, cv_src, cv_dst, cv_rows)

    def x_copy(tile):
        return pltpu.make_async_copy(x_hbm.at[pl.ds(tile * tm, tm), :], xf_ref, x_sem)

    def gate_up(slot):
        xb = xb_ref[...]
        gate = _dot(xb, wg_ref[...])
        up = _dot(xb, wu_ref[...])
        act_ref[slot] = (gate / (1.0 + jnp.exp(-gate)) * up).astype(BF16)

    def down(slot, rows=slice(None)):
        o_ref[rows, :] += _dot(act_ref[slot, rows, :], wo_ref[...])

    @pl.when(jnp.logical_and(i == 0, j == 0))
    def _():
        x_copy(0).start()

    @pl.when(j == 0)
    def _():
        x_copy(i).wait()
        x = xf_ref[...]
        xb_ref[...] = x.astype(BF16)
        o_ref[...] = (2.0 * alpha) * x
        gate_up(0)

    @pl.when(jnp.logical_and(j == 1, i + 1 < pl.num_programs(0)))
    def _():
        x_copy(i + 1).start()

    for parity in range(2):
        @pl.when(jnp.logical_and(jnp.logical_and(j > 0, j < nf), j % 2 == parity))
        def _(parity=parity):
            gate_up(parity)
            down(1 - parity)

    @pl.when(j == nf)
    def _():
        for r in range(2):
            rows = slice(r * (tm // 2), (r + 1) * (tm // 2))
            down((nf - 1) % 2, rows)
            o_ref[rows, :] = _layer_norm_rows(0.5 * o_ref[rows, :], g_ref[...], b_ref[...])


def _ffn(x, w_in, w_out, g, b, *, alpha, tm, tf, convert=()):
    m, d = x.shape
    f = w_out.shape[0]
    assert m % tm == 0 and f % tf == 0 and tm % (2 * SUBLANES) == 0
    nf = f // tf
    heights = _convert_heights(convert, (m // tm) * (nf + 1))
    cv_specs, cv_shapes, cv_vmem = _convert_specs(convert, heights, lambda i, j: i * (nf + 1) + j)
    vmem = (tm * d * F32_BYTES
            + 2 * tm * d * F32_BYTES
            + tm * d * BF16_BYTES
            + 2 * tm * tf * BF16_BYTES
            + 2 * 3 * d * tf * BF16_BYTES
            + 2 * tm * tf * F32_BYTES
            + (tm // 2) * d * F32_BYTES
            + cv_vmem)
    return pl.pallas_call(
        functools.partial(_ffn_kernel, alpha=alpha, nf=nf, tm=tm, cv_rows=tuple(a.shape[0] for a in convert)),
        grid=(m // tm, nf + 1),
        in_specs=[
            pl.BlockSpec(memory_space=pl.ANY),
            pl.BlockSpec((d, tf), lambda i, j: (0, jnp.minimum(j, nf - 1))),
            pl.BlockSpec((d, tf), lambda i, j: (0, jnp.minimum(j, nf - 1) + nf)),
            pl.BlockSpec((tf, d), lambda i, j: (jnp.maximum(j - 1, 0), 0)),
            pl.BlockSpec((1, d), lambda i, j: (0, 0)),
            pl.BlockSpec((1, d), lambda i, j: (0, 0)),
            *cv_specs,
        ],
        out_specs=[pl.BlockSpec((tm, d), lambda i, j: (i, 0)), *cv_specs],
        out_shape=[jax.ShapeDtypeStruct((m, d), F32), *cv_shapes],
        scratch_shapes=[pltpu.VMEM((tm, d), F32), pltpu.VMEM((tm, d), BF16), pltpu.VMEM((2, tm, tf), BF16),
                        pltpu.SemaphoreType.DMA],
        compiler_params=pltpu.CompilerParams(
            dimension_semantics=("arbitrary", "arbitrary"),
            vmem_limit_bytes=_vmem_limit(vmem)),
        name="ffn",
    )(x, w_in, w_in, w_out, g, b, *convert)


def _rotate_half(x):
    lane = lax.broadcasted_iota(jnp.int32, x.shape, 1)
    first_half = (lane % HEAD_DIM) < (HEAD_DIM // 2)
    return jnp.where(first_half, pltpu.roll(x, LANES - HEAD_DIM // 2, 1), pltpu.roll(x, HEAD_DIM // 2, 1))


def _mix_kernel(sink_ref, x_ref, w_ref, wo_ref, cw_ref, gm_ref, g_ref, b_ref, cos_ref, sin_ref,
                ck_ref, cv_ref, cc_ref, *rest,
                nb, tq, n_t, n_tiles, d_conv, d_attn, mask_halo, alpha, cv_rows):
    n_cv = len(cv_rows)
    cv_src, (o_ref, nconv_ref, nk_ref, nv_ref) = rest[:n_cv], rest[n_cv:n_cv + 4]
    cv_dst = rest[n_cv + 4:2 * n_cv + 4]
    kext, vext, k2, v2, uext, qs, resid, ya, yc = rest[2 * n_cv + 4:]
    s = pl.program_id(0)
    _convert_step(s, cv_src, cv_dst, cv_rows)
    d_kv = N_KV_HEADS * HEAD_DIM
    gqa = d_attn // HEAD_DIM // N_KV_HEADS
    pad = SUBLANES
    tail = CONV_WIDTH - 1
    rows = nb * tq
    band = WINDOW + CHUNK
    ext = WINDOW + tq
    n_chunks = tq // CHUNK
    n_pairs = d_attn // LANES
    q_off = 3 * d_conv

    def stage_a(slot, t):
        first = t == 0
        x = x_ref[...].reshape(rows, x_ref.shape[-1])
        xb = x.astype(BF16)
        half = d_conv // 2
        col = lambda g, hf: slice(g * d_conv + hf * half, g * d_conv + (hf + 1) * half)

        b_gate = []
        for hf in range(2):
            b_gate.append(_dot(xb, w_ref[:, col(0, hf)]))
            yield
        c_gate = []
        for hf in range(2):
            c_gate.append(_dot(xb, w_ref[:, col(1, hf)]))
            yield
        y_half = []
        for hf in range(2):
            ch = slice(hf * half, (hf + 1) * half)
            hc = _dot(xb, w_ref[:, col(2, hf)])
            u = (c_gate[hf] * hc).reshape(nb, tq, half)
            uext[:, pad - tail:pad, ch] = jnp.where(first, cc_ref[:, :, ch], uext[:, pad + tq - tail:pad + tq, ch])
            uext[:, pad:pad + tq, ch] = u
            z = cw_ref[CONV_WIDTH - 1:CONV_WIDTH, ch] * u
            for jj in range(tail):
                off = pad - tail + jj
                z = z + cw_ref[jj:jj + 1, ch] * uext[:, off:off + tq, ch]
            y_half.append(b_gate[hf].reshape(nb, tq, half) * z)
            if hf == 0:
                yield
        nconv_ref[...] = uext[:, pad + tq - tail:pad + tq, :]
        ssq = sum(jnp.sum(y * y, axis=-1, keepdims=True) for y in y_half)
        inv = lax.rsqrt(ssq * (1.0 / d_conv) + RMS_EPS)
        for hf in range(2):
            ch = slice(hf * half, (hf + 1) * half)
            yc[:, ch] = (y_half[hf] * inv * gm_ref[:, ch]).reshape(rows, half).astype(BF16)
        yield

        cos = cos_ref[...]
        sin = sin_ref[...]
        kext[slot, :, 0:WINDOW, :] = jnp.where(first, ck_ref[...], kext[1 - slot, :, tq:ext, :])
        vext[slot, :, 0:WINDOW, :] = jnp.where(first, cv_ref[...], vext[1 - slot, :, tq:ext, :])
        kv = _dot(xb, w_ref[:, q_off + d_attn:q_off + d_attn + 2 * d_kv]).reshape(nb, tq, 2 * d_kv)
        for bi in range(nb):
            k_b = kv[bi, :, 0:d_kv]
            kext[slot, bi, WINDOW:ext, :] = k_b * cos + _rotate_half(k_b) * sin
            vext[slot, bi, WINDOW:ext, :] = kv[bi, :, d_kv:2 * d_kv]
        nk_ref[...] = kext[slot, :, tq:ext, :]
        nv_ref[...] = vext[slot, :, tq:ext, :]

        low_half_ext = lax.broadcasted_iota(jnp.int32, (ext, LANES), 1) < HEAD_DIM
        for bi in range(nb):
            for src, dst in ((kext, k2), (vext, v2)):
                rows_f = src[slot, bi]
                rows_sw = pltpu.roll(rows_f, HEAD_DIM, 1)
                dst[slot, 0, bi] = jnp.where(low_half_ext, rows_f, rows_sw).astype(BF16)
                dst[slot, 1, bi] = jnp.where(low_half_ext, rows_sw, rows_f).astype(BF16)
        yield

        cos_q = cos * ATTN_SCALE
        sin_q = sin * ATTN_SCALE
        for hf in range(2):
            lo = hf * (d_attn // 2)
            q = _dot(xb, w_ref[:, q_off + lo:q_off + lo + d_attn // 2]).reshape(nb, tq, d_attn // 2)
            for bi in range(nb):
                for p in range(n_pairs // 2):
                    qp = q[bi, :, p * LANES:(p + 1) * LANES]
                    qs[slot, bi, :, lo + p * LANES:lo + (p + 1) * LANES] = (
                        qp * cos_q + _rotate_half(qp) * sin_q).astype(BF16)
            yield

        resid[slot] = alpha * x + _dot(yc[...], wo_ref[0:d_conv, :])

    def stage_b(slot, t):
        low_half = lax.broadcasted_iota(jnp.int32, (CHUNK, LANES), 1) < HEAD_DIM
        low_half_row = lax.broadcasted_iota(jnp.int32, (1, LANES), 1) < HEAD_DIM
        zero_q = jnp.zeros((CHUNK, LANES), BF16)
        for bi in range(nb):
            for c in range(n_chunks):
                r0 = c * CHUNK
                masked = mask_halo and c < WINDOW // CHUNK
                if masked:
                    key_pos = (t * n_chunks + (c - WINDOW // CHUNK)) * CHUNK + lax.broadcasted_iota(
                        jnp.int32, (band, 1), 0)
                    valid = key_pos >= 0
                y_blocks = [None] * n_pairs
                for h in range(N_KV_HEADS):
                    q_stack = []
                    sink_cols = []
                    for gi in range(gqa):
                        n = h * gqa + gi
                        qp = qs[slot, bi, r0:r0 + CHUNK, (n // 2) * LANES:(n // 2 + 1) * LANES]
                        keep = low_half if n % 2 == 0 else jnp.logical_not(low_half)
                        q_stack.append(jnp.where(keep, qp, zero_q))
                        if gi % 2 == 0:
                            sink_cols.append(jnp.where(low_half_row, sink_ref[n], sink_ref[n + 1]))
                    qm = jnp.concatenate(q_stack, axis=0)
                    sink = jnp.concatenate(sink_cols, axis=1)
                    sc = lax.dot_general(k2[slot, h, bi, r0:r0 + band, :], qm, (((1,), (1,)), ((), ())),
                                         preferred_element_type=F32)
                    if masked:
                        sc = jnp.where(valid, sc, MASK_VALUE)
                    m_col = jnp.maximum(jnp.max(sc, axis=0, keepdims=True), sink)
                    e = jnp.exp(sc - m_col)
                    inv_denom = 1.0 / (jnp.sum(e, axis=0, keepdims=True) + jnp.exp(sink - m_col))
                    e = e.astype(BF16)
                    yield
                    o_t = lax.dot_general(v2[slot, h, bi, r0:r0 + band, :], e,
                                          (((0,), (0,)), ((), ())), preferred_element_type=F32)
                    o = (o_t * inv_denom).T
                    for jp in range(gqa // 2):
                        o_even = o[(2 * jp) * CHUNK:(2 * jp + 1) * CHUNK, :]
                        o_odd = o[(2 * jp + 1) * CHUNK:(2 * jp + 2) * CHUNK, :]
                        y_blocks[(h * gqa) // 2 + jp] = jnp.where(low_half, o_even, o_odd)
                y_attn = jnp.concatenate(y_blocks, axis=1)
                ya[bi, r0:r0 + CHUNK, :] = _rms_norm_rows(
                    y_attn, gm_ref[:, d_conv:d_conv + d_attn]).astype(BF16)
        r = resid[slot] + _dot(ya[...].reshape(rows, d_attn), wo_ref[d_conv:d_conv + d_attn, :])
        o_ref[...] = _layer_norm_rows(r, g_ref[...], b_ref[...]).reshape(o_ref.shape)

    def run(*stages):
        live = list(stages)
        while live:
            for g in list(live):
                if next(g, StopIteration) is StopIteration:
                    live.remove(g)

    @pl.when(s == 0)
    def _():
        run(stage_a(0, 0))

    for parity in range(2):
        @pl.when(jnp.logical_and(jnp.logical_and(s > 0, s < n_tiles), s % 2 == parity))
        def _(parity=parity):
            run(stage_b(1 - parity, (s - 1) % n_t), stage_a(parity, s % n_t))

    @pl.when(s == n_tiles)
    def _():
        run(stage_b((n_tiles - 1) % 2, (n_tiles - 1) % n_t))


def _mix(x, w_in, w_out, conv_w, sinks, norm_g, ln_g, ln_b, cos, sin, cache_k, cache_v, cache_conv,
         *, nb, tq, mask_halo, alpha, convert=()):
    bsz, t_len, d = x.shape
    d_conv = conv_w.shape[1]
    d_mix = norm_g.shape[1]
    d_attn = d_mix - d_conv
    d_kv = N_KV_HEADS * HEAD_DIM
    assert bsz % nb == 0 and t_len % tq == 0 and tq % CHUNK == 0
    assert d_kv == LANES and d_attn % LANES == 0 and (d_attn // HEAD_DIM) % (2 * N_KV_HEADS) == 0
    assert w_in.shape[1] == 3 * d_conv + d_attn + 2 * d_kv
    n_t = t_len // tq
    assert n_t == 1 or tq >= WINDOW
    n_tiles = (bsz // nb) * n_t
    rows = nb * tq
    ext = WINDOW + tq
    heights = _convert_heights(convert, n_tiles + 1)
    cv_specs, cv_shapes, cv_vmem = _convert_specs(convert, heights, lambda s: s)
    vmem = (w_in.size * BF16_BYTES + w_out.size * BF16_BYTES
            + 2 * 2 * rows * d * F32_BYTES
            + 2 * 2 * tq * LANES * F32_BYTES
            + 2 * 2 * nb * ext * d_kv * F32_BYTES
            + 2 * 2 * N_KV_HEADS * nb * ext * LANES * BF16_BYTES
            + nb * (SUBLANES + tq) * d_conv * F32_BYTES
            + 3 * rows * d_attn * BF16_BYTES
            + rows * d_conv * BF16_BYTES
            + 2 * rows * d * F32_BYTES
            + 4 * 2 * nb * WINDOW * d_kv * F32_BYTES
            + 4 * rows * d_conv * F32_BYTES
            + 2 * rows * d * F32_BYTES
            + cv_vmem)
    kernel = functools.partial(_mix_kernel, nb=nb, tq=tq, n_t=n_t, n_tiles=n_tiles, d_conv=d_conv,
                               d_attn=d_attn, mask_halo=mask_halo, alpha=alpha,
                               cv_rows=tuple(a.shape[0] for a in convert))
    tile_a = lambda s: jnp.minimum(s, n_tiles - 1)
    tile_b = lambda s: jnp.maximum(s - 1, 0)
    const = lambda s: (0, 0)
    seq_a = lambda s: (tile_a(s) // n_t, 0, 0)
    return pl.pallas_call(
        kernel,
        grid=(n_tiles + 1,),
        in_specs=[
            pl.BlockSpec(memory_space=pltpu.SMEM),
            pl.BlockSpec((nb, tq, d), lambda s: (tile_a(s) // n_t, tile_a(s) % n_t, 0)),
            pl.BlockSpec(w_in.shape, const, pipeline_mode=pl.Buffered(1)),
            pl.BlockSpec(w_out.shape, const, pipeline_mode=pl.Buffered(1)),
            pl.BlockSpec(conv_w.shape, const),
            pl.BlockSpec(norm_g.shape, const),
            pl.BlockSpec(ln_g.shape, const),
            pl.BlockSpec(ln_b.shape, const),
            pl.BlockSpec((tq, LANES), lambda s: (tile_a(s) % n_t, 0)),
            pl.BlockSpec((tq, LANES), lambda s: (tile_a(s) % n_t, 0)),
            pl.BlockSpec((nb, WINDOW, d_kv), seq_a),
            pl.BlockSpec((nb, WINDOW, d_kv), seq_a),
            pl.BlockSpec((nb, CONV_WIDTH - 1, d_conv), seq_a),
            *cv_specs,
        ],
        out_specs=[
            pl.BlockSpec((nb, tq, d), lambda s: (tile_b(s) // n_t, tile_b(s) % n_t, 0)),
            pl.BlockSpec((nb, CONV_WIDTH - 1, d_conv), seq_a),
            pl.BlockSpec((nb, WINDOW, d_kv), seq_a),
            pl.BlockSpec((nb, WINDOW, d_kv), seq_a),
            *cv_specs,
        ],
        out_shape=[
            jax.ShapeDtypeStruct((bsz, t_len, d), F32),
            jax.ShapeDtypeStruct((bsz, CONV_WIDTH - 1, d_conv), F32),
            jax.ShapeDtypeStruct((bsz, WINDOW, d_kv), F32),
            jax.ShapeDtypeStruct((bsz, WINDOW, d_kv), F32),
            *cv_shapes,
        ],
        scratch_shapes=[
            pltpu.VMEM((2, nb, ext, d_kv), F32),
            pltpu.VMEM((2, nb, ext, d_kv), F32),
            pltpu.VMEM((2, N_KV_HEADS, nb, ext, LANES), BF16),
            pltpu.VMEM((2, N_KV_HEADS, nb, ext, LANES), BF16),
            pltpu.VMEM((nb, SUBLANES + tq, d_conv), F32),
            pltpu.VMEM((2, nb, tq, d_attn), BF16),
            pltpu.VMEM((2, rows, d), F32),
            pltpu.VMEM((nb, tq, d_attn), BF16),
            pltpu.VMEM((rows, d_conv), BF16),
        ],
        compiler_params=pltpu.CompilerParams(
            dimension_semantics=("arbitrary",),
            vmem_limit_bytes=_vmem_limit(vmem)),
        name="mix",
    )(sinks, x, w_in, w_out, conv_w, norm_g, ln_g, ln_b, cos, sin, cache_k, cache_v, cache_conv, *convert)


def _rope_tables(pos):
    half = HEAD_DIM // 2
    inv = ROPE_THETA ** (-jnp.arange(half, dtype=F32) / half)
    ang = pos.astype(F32)[:, None] * inv[None, :]
    cos = jnp.cos(ang)
    sin = jnp.sin(ang)
    reps = LANES // HEAD_DIM
    return jnp.tile(jnp.concatenate([cos, cos], axis=1), (1, reps)), jnp.tile(jnp.concatenate([-sin, sin], axis=1), (1, reps))


def _largest_tile(n, target):
    t = min(n, target)
    while n % t:
        t //= 2
    return t


def _plan(bp, sp, bs, ts):
    ffn_rows = 1024
    ffn_chunk = 512
    mix_rows = 256
    return dict(
        tm_p=_largest_tile(bp * sp, ffn_rows), tm_s=_largest_tile(bs * ts, ffn_rows), tf=ffn_chunk,
        mix_p=dict(nb=1, tq=_largest_tile(sp, mix_rows)),
        mix_s=dict(nb=_largest_tile(bs, mix_rows // ts), tq=ts))


def kernel(x_prompt, x_sample, cache_conv, cache_k, cache_v, ffn1_w_in, ffn1_w_out, ln1_g, ln1_b, w_mix_in, conv_w, attn_sinks, mix_norm_g, w_mix_out, ln2_g, ln2_b, ffn2_w_in, ffn2_w_out, ln3_g, ln3_b):
    depth = ffn1_w_in.shape[0]
    bp, sp, d = x_prompt.shape
    bs, ts, _ = x_sample.shape
    assert ts == CHUNK
    alpha = (2.0 * depth) ** 0.25
    d_conv = conv_w.shape[-1]
    d_kv = N_KV_HEADS * HEAD_DIM
    plan = _plan(bp, sp, bs, ts)
    tf = _largest_tile(ffn1_w_out.shape[1], plan["tf"])

    cos_p, sin_p = _rope_tables(jnp.arange(sp))
    cos_s, sin_s = _rope_tables(PAST_LEN + jnp.arange(ts))
    zeros_conv = jnp.zeros((bp, CONV_WIDTH - 1, d_conv), F32)
    zeros_kv = jnp.zeros((bp, WINDOW, d_kv), F32)

    xp = x_prompt.reshape(bp * sp, d)
    xs = x_sample.reshape(bs * ts, d)
    outs = {k: [] for k in ("conv_p", "k_p", "v_p", "conv_s", "k_s", "v_s")}
    for l in range(depth):
        w1_in, w1_out = ffn1_w_in[l].astype(BF16), ffn1_w_out[l].astype(BF16)
        g1, b1 = ln1_g[l][None, :], ln1_b[l][None, :]
        g2, b2 = ln2_g[l][None, :], ln2_b[l][None, :]
        g3, b3 = ln3_g[l][None, :], ln3_b[l][None, :]
        gm = mix_norm_g[l][None, :]

        xp, wm_in, wm_out = _ffn(xp, w1_in, w1_out, g1, b1, alpha=alpha, tm=plan["tm_p"], tf=tf,
                                 convert=(w_mix_in[l], w_mix_out[l]))
        xs, = _ffn(xs, w1_in, w1_out, g1, b1, alpha=alpha, tm=plan["tm_s"], tf=tf)

        xp, conv_p, k_p, v_p, w2_in, w2_out = _mix(
            xp.reshape(bp, sp, d), wm_in, wm_out, conv_w[l], attn_sinks[l], gm, g2, b2, cos_p, sin_p,
            zeros_kv, zeros_kv, zeros_conv, mask_halo=True, alpha=alpha, **plan["mix_p"],
            convert=(ffn2_w_in[l], ffn2_w_out[l]))
        xs, conv_s, k_s, v_s = _mix(
            xs.reshape(bs, ts, d), wm_in, wm_out, conv_w[l], attn_sinks[l], gm, g2, b2, cos_s, sin_s,
            cache_k[l].reshape(bs, WINDOW, d_kv), cache_v[l].reshape(bs, WINDOW, d_kv), cache_conv[l],
            mask_halo=False, alpha=alpha, **plan["mix_s"])
        xp = xp.reshape(bp * sp, d)
        xs = xs.reshape(bs * ts, d)

        xp, = _ffn(xp, w2_in, w2_out, g3, b3, alpha=alpha, tm=plan["tm_p"], tf=tf)
        xs, = _ffn(xs, w2_in, w2_out, g3, b3, alpha=alpha, tm=plan["tm_s"], tf=tf)

        kv_shape_p = (bp, WINDOW, N_KV_HEADS, HEAD_DIM)
        kv_shape_s = (bs, WINDOW, N_KV_HEADS, HEAD_DIM)
        outs["conv_p"].append(conv_p)
        outs["k_p"].append(k_p.reshape(kv_shape_p))
        outs["v_p"].append(v_p.reshape(kv_shape_p))
        outs["conv_s"].append(conv_s)
        outs["k_s"].append(k_s.reshape(kv_shape_s))
        outs["v_s"].append(v_s.reshape(kv_shape_s))

    return (xp.reshape(bp, sp, d), xs.reshape(bs, ts, d),
            jnp.stack(outs["conv_p"]), jnp.stack(outs["k_p"]), jnp.stack(outs["v_p"]),
            jnp.stack(outs["conv_s"]), jnp.stack(outs["k_s"]), jnp.stack(outs["v_s"]))
```

```python
import functools

import jax
import jax.numpy as jnp
from jax import lax
from jax.experimental import pallas as pl
from jax.experimental.pallas import tpu as pltpu

CHUNK = 64
HEAD_DIM = 64
N_KV_HEADS = 2
WINDOW = 128
CONV_WIDTH = 3
PAST_LEN = 2048
ROPE_THETA = 10000.0
LN_EPS = 1e-5
RMS_EPS = 1e-6
ATTN_SCALE = HEAD_DIM ** -0.5
MASK_VALUE = -1e30

LANES = 128
SUBLANES = 8
V7X_VMEM_BYTES = 64 * 1024 * 1024
V7X_VMEM_REQUEST_CAP = V7X_VMEM_BYTES - 6 * 1024 * 1024

F32 = jnp.float32
BF16 = jnp.bfloat16
F32_BYTES = 4
BF16_BYTES = 2
BF16_ROW_TILE = 2 * SUBLANES


def _dot(a, b):
    return jnp.dot(a, b, preferred_element_type=F32)


def _layer_norm_rows(y, g, b, scale=1.0):
    mu = jnp.mean(y, axis=-1, keepdims=True)
    d = y - mu
    var = jnp.mean(d * d, axis=-1, keepdims=True)
    return d * lax.rsqrt(var + (scale * scale) * LN_EPS) * g + b


def _rms_norm_rows(y, g):
    inv = lax.rsqrt(jnp.mean(y * y, axis=-1, keepdims=True) + RMS_EPS)
    return y * inv * g


def _vmem_limit(n_bytes):
    assert n_bytes <= V7X_VMEM_REQUEST_CAP, n_bytes
    return int(n_bytes)


def _convert_heights(arrays, n_steps):
    heights = []
    for a in arrays:
        r = a.shape[0]
        heights.append(next(h for h in range(BF16_ROW_TILE, r + 1, BF16_ROW_TILE)
                            if r % h == 0 and r // h <= n_steps))
    return heights


def _convert_specs(arrays, heights, step_of):
    specs, shapes, vmem = [], [], 0
    for a, h in zip(arrays, heights):
        last = a.shape[0] // h - 1
        specs.append(pl.BlockSpec((h, a.shape[1]), lambda *idx, last=last: (jnp.minimum(step_of(*idx), last), 0)))
        shapes.append(jax.ShapeDtypeStruct(a.shape, BF16))
        vmem += 2 * h * a.shape[1] * (F32_BYTES + BF16_BYTES)
    return specs, shapes, vmem


def _convert_blocks(src_refs, dst_refs):
    for src, dst in zip(src_refs, dst_refs):
        dst[...] = src[...].astype(BF16)


def _ffn_kernel(x_hbm, wg_ref, wu_ref, wo_ref, g_ref, b_ref, *rest, alpha, nf, tm, n_cv, cast_weights):
    cv_src, o_ref, cv_dst = rest[:n_cv], rest[n_cv], rest[n_cv + 1:2 * n_cv + 1]
    rest = rest[2 * n_cv + 1:]
    w_outs = w_bufs = (None, None, None)
    if cast_weights:
        w_outs, rest = rest[:3], rest[3:]
    xf_ref, xb_ref, act_ref, x_sem = rest[:4]
    if cast_weights:
        w_bufs = rest[4:]
    i = pl.program_id(0)
    j = pl.program_id(1)
    convert = functools.partial(_convert_blocks, cv_src, cv_dst)

    def x_copy(tile):
        return pltpu.make_async_copy(x_hbm.at[pl.ds(tile * tm, tm), :], xf_ref, x_sem)

    def weight(ref, k):
        if not cast_weights:
            return ref[...]
        w_bufs[k][...] = ref[...].astype(BF16)
        return w_bufs[k][...]

    def gate_up(slot):
        xb = xb_ref[...]
        gate = _dot(xb, weight(wg_ref, 0))
        up = _dot(xb, weight(wu_ref, 1))
        act_ref[slot] = (gate / (1.0 + jnp.exp(-gate)) * up).astype(BF16)

    def down(slot, w, rows=slice(None)):
        o_ref[rows, :] += _dot(act_ref[slot, rows, :], w)

    @pl.when(jnp.logical_and(i == 0, j == 0))
    def _():
        x_copy(0).start()

    @pl.when(j == 0)
    def _():
        convert()
        x_copy(i).wait()
        x = xf_ref[...]
        xb_ref[...] = x.astype(BF16)
        o_ref[...] = (2.0 * alpha) * x
        gate_up(0)

    @pl.when(jnp.logical_and(j == 1, i + 1 < pl.num_programs(0)))
    def _():
        x_copy(i + 1).start()

    for parity in range(2):
        @pl.when(jnp.logical_and(jnp.logical_and(j > 0, j < nf), j % 2 == parity))
        def _(parity=parity):
            convert()
            gate_up(parity)
            down(1 - parity, weight(wo_ref, 2))

    @pl.when(j == nf)
    def _():
        convert()
        w = weight(wo_ref, 2)
        for r in range(2):
            rows = slice(r * (tm // 2), (r + 1) * (tm // 2))
            down((nf - 1) % 2, w, rows)
            o_ref[rows, :] = _layer_norm_rows(o_ref[rows, :], g_ref[...], b_ref[...], scale=2.0)

    if cast_weights:
        @pl.when(i == 0)
        def _():
            w_outs[0][...] = w_bufs[0][...]
            w_outs[1][...] = w_bufs[1][...]

        @pl.when(jnp.logical_and(i == 0, j > 0))
        def _():
            w_outs[2][...] = w_bufs[2][...]


def _ffn(x, w_gate, w_up, w_out, g, b, *, alpha, tm, tf, convert=(), cast_weights=False):
    m, d = x.shape
    f = w_out.shape[0]
    assert m % tm == 0 and f % tf == 0 and tm % (2 * SUBLANES) == 0
    nf = f // tf
    up_offset = nf if w_up.shape[1] == 2 * f else 0
    heights = _convert_heights(convert, (m // tm) * (nf + 1))
    cv_specs, cv_shapes, cv_vmem = _convert_specs(convert, heights, lambda i, j: i * (nf + 1) + j)
    w_bytes = F32_BYTES + BF16_BYTES if cast_weights else BF16_BYTES
    vmem = ((3 * d * tf * BF16_BYTES if cast_weights else 0)
            + tm * d * F32_BYTES
            + 2 * tm * d * F32_BYTES
            + tm * d * BF16_BYTES
            + 2 * tm * tf * BF16_BYTES
            + 2 * 3 * d * tf * w_bytes
            + 2 * tm * tf * F32_BYTES
            + (tm // 2) * d * F32_BYTES
            + cv_vmem)
    gate_spec = pl.BlockSpec((d, tf), lambda i, j: (0, jnp.minimum(j, nf - 1)))
    up_spec = pl.BlockSpec((d, tf), lambda i, j: (0, jnp.minimum(j, nf - 1) + up_offset))
    down_spec = pl.BlockSpec((tf, d), lambda i, j: (jnp.maximum(j - 1, 0), 0))
    w_specs = [gate_spec, up_spec, down_spec]
    w_out_specs, w_out_shapes, w_scratch = [], [], []
    if cast_weights:
        in_out = pl.BlockSpec((d, tf), lambda i, j: (0, jnp.where(i == 0, jnp.minimum(j, nf - 1), nf - 1)))
        down_out = pl.BlockSpec((tf, d), lambda i, j: (jnp.where(i == 0, jnp.maximum(j - 1, 0), nf - 1), 0))
        w_out_specs = [in_out, in_out, down_out]
        w_out_shapes = [jax.ShapeDtypeStruct((d, f), BF16)] * 2 + [jax.ShapeDtypeStruct(w_out.shape, BF16)]
        w_scratch = [pltpu.VMEM((d, tf), BF16), pltpu.VMEM((d, tf), BF16), pltpu.VMEM((tf, d), BF16)]
    return pl.pallas_call(
        functools.partial(_ffn_kernel, alpha=alpha, nf=nf, tm=tm, n_cv=len(convert), cast_weights=cast_weights),
        grid=(m // tm, nf + 1),
        in_specs=[
            pl.BlockSpec(memory_space=pl.ANY),
            *w_specs,
            pl.BlockSpec((1, d), lambda i, j: (0, 0)),
            pl.BlockSpec((1, d), lambda i, j: (0, 0)),
            *cv_specs,
        ],
        out_specs=[pl.BlockSpec((tm, d), lambda i, j: (i, 0)), *cv_specs, *w_out_specs],
        out_shape=[jax.ShapeDtypeStruct((m, d), F32), *cv_shapes, *w_out_shapes],
        scratch_shapes=[pltpu.VMEM((tm, d), F32), pltpu.VMEM((tm, d), BF16), pltpu.VMEM((2, tm, tf), BF16),
                        pltpu.SemaphoreType.DMA, *w_scratch],
        compiler_params=pltpu.CompilerParams(
            dimension_semantics=("arbitrary", "arbitrary"),
            vmem_limit_bytes=_vmem_limit(vmem)),
        name="ffn",
    )(x, w_gate, w_up, w_out, g, b, *convert)


def _rotate_half(x):
    lane = lax.broadcasted_iota(jnp.int32, x.shape, 1)
    first_half = (lane % HEAD_DIM) < (HEAD_DIM // 2)
    return jnp.where(first_half, pltpu.roll(x, LANES - HEAD_DIM // 2, 1), pltpu.roll(x, HEAD_DIM // 2, 1))


def _mix_kernel(sink_ref, x_ref, w_ref, wo_ref, cw_ref, gm_ref, g_ref, b_ref, cos_ref, sin_ref,
                ck_ref, cv_ref, cc_ref, *rest,
                nb, tq, n_t, n_tiles, d_conv, d_attn, mask_halo, alpha, n_cv):
    cv_src, (o_ref, nconv_ref, nk_ref, nv_ref) = rest[:n_cv], rest[n_cv:n_cv + 4]
    cv_dst = rest[n_cv + 4:2 * n_cv + 4]
    kext, vext, k2, v2, uext, qs, resid, ya, yc = rest[2 * n_cv + 4:]
    s = pl.program_id(0)
    convert = functools.partial(_convert_blocks, cv_src, cv_dst)
    d_kv = N_KV_HEADS * HEAD_DIM
    gqa = d_attn // HEAD_DIM // N_KV_HEADS
    pad = SUBLANES
    tail = CONV_WIDTH - 1
    rows = nb * tq
    band = WINDOW + CHUNK
    ext = WINDOW + tq
    n_chunks = tq // CHUNK
    n_pairs = d_attn // LANES
    q_off = 3 * d_conv

    def stage_a(slot, t):
        first = t == 0
        x = x_ref[...].reshape(rows, x_ref.shape[-1])
        xb = x.astype(BF16)
        half = d_conv // 2
        col = lambda g, hf: slice(g * d_conv + hf * half, g * d_conv + (hf + 1) * half)

        b_gate = []
        for hf in range(2):
            b_gate.append(_dot(xb, w_ref[:, col(0, hf)]))
            yield
        c_gate = []
        for hf in range(2):
            c_gate.append(_dot(xb, w_ref[:, col(1, hf)]))
            yield
        y_half = []
        for hf in range(2):
            ch = slice(hf * half, (hf + 1) * half)
            hc = _dot(xb, w_ref[:, col(2, hf)])
            u = (c_gate[hf] * hc).reshape(nb, tq, half)
            uext[:, pad - tail:pad, ch] = jnp.where(first, cc_ref[:, :, ch], uext[:, pad + tq - tail:pad + tq, ch])
            uext[:, pad:pad + tq, ch] = u
            z = cw_ref[CONV_WIDTH - 1:CONV_WIDTH, ch] * u
            for jj in range(tail):
                off = pad - tail + jj
                z = z + cw_ref[jj:jj + 1, ch] * uext[:, off:off + tq, ch]
            y_half.append(b_gate[hf].reshape(nb, tq, half) * z)
            if hf == 0:
                yield
        nconv_ref[...] = uext[:, pad + tq - tail:pad + tq, :]
        ssq = sum(jnp.sum(y * y, axis=-1, keepdims=True) for y in y_half)
        inv = lax.rsqrt(ssq * (1.0 / d_conv) + RMS_EPS)
        for hf in range(2):
            ch = slice(hf * half, (hf + 1) * half)
            yc[:, ch] = (y_half[hf] * inv * gm_ref[:, ch]).reshape(rows, half).astype(BF16)
        yield

        cos = cos_ref[...]
        sin = sin_ref[...]
        kext[slot, :, 0:WINDOW, :] = jnp.where(first, ck_ref[...], kext[1 - slot, :, tq:ext, :])
        vext[slot, :, 0:WINDOW, :] = jnp.where(first, cv_ref[...], vext[1 - slot, :, tq:ext, :])
        kv = _dot(xb, w_ref[:, q_off + d_attn:q_off + d_attn + 2 * d_kv]).reshape(nb, tq, 2 * d_kv)
        for bi in range(nb):
            k_b = kv[bi, :, 0:d_kv]
            kext[slot, bi, WINDOW:ext, :] = k_b * cos + _rotate_half(k_b) * sin
            vext[slot, bi, WINDOW:ext, :] = kv[bi, :, d_kv:2 * d_kv]
        nk_ref[...] = kext[slot, :, tq:ext, :]
        nv_ref[...] = vext[slot, :, tq:ext, :]

        low_half_ext = lax.broadcasted_iota(jnp.int32, (ext, LANES), 1) < HEAD_DIM
        for bi in range(nb):
            for src, dst in ((kext, k2), (vext, v2)):
                rows_f = src[slot, bi]
                rows_sw = pltpu.roll(rows_f, HEAD_DIM, 1)
                dst[slot, 0, bi] = jnp.where(low_half_ext, rows_f, rows_sw).astype(BF16)
                dst[slot, 1, bi] = jnp.where(low_half_ext, rows_sw, rows_f).astype(BF16)
        yield

        cos_q = cos * ATTN_SCALE
        sin_q = sin * ATTN_SCALE
        for hf in range(2):
            lo = hf * (d_attn // 2)
            q = _dot(xb, w_ref[:, q_off + lo:q_off + lo + d_attn // 2]).reshape(nb, tq, d_attn // 2)
            for bi in range(nb):
                for p in range(n_pairs // 2):
                    qp = q[bi, :, p * LANES:(p + 1) * LANES]
                    qs[slot, bi, :, lo + p * LANES:lo + (p + 1) * LANES] = (
                        qp * cos_q + _rotate_half(qp) * sin_q).astype(BF16)
            yield

        resid[slot] = alpha * x + _dot(yc[...], wo_ref[0:d_conv, :])

    def stage_b(slot, t):
        low_half = lax.broadcasted_iota(jnp.int32, (CHUNK, LANES), 1) < HEAD_DIM
        low_half_row = lax.broadcasted_iota(jnp.int32, (1, LANES), 1) < HEAD_DIM
        zero_q = jnp.zeros((CHUNK, LANES), BF16)
        for bi in range(nb):
            for c in range(n_chunks):
                r0 = c * CHUNK
                masked = mask_halo and c < WINDOW // CHUNK
                if masked:
                    key_pos = (t * n_chunks + (c - WINDOW // CHUNK)) * CHUNK + lax.broadcasted_iota(
                        jnp.int32, (band, 1), 0)
                    valid = key_pos >= 0
                y_blocks = [None] * n_pairs
                for h in range(N_KV_HEADS):
                    q_stack = []
                    sink_cols = []
                    for gi in range(gqa):
                        n = h * gqa + gi
                        qp = qs[slot, bi, r0:r0 + CHUNK, (n // 2) * LANES:(n // 2 + 1) * LANES]
                        keep = low_half if n % 2 == 0 else jnp.logical_not(low_half)
                        q_stack.append(jnp.where(keep, qp, zero_q))
                        if gi % 2 == 0:
                            sink_cols.append(jnp.where(low_half_row, sink_ref[n], sink_ref[n + 1]))
                    qm = jnp.concatenate(q_stack, axis=0)
                    sink = jnp.concatenate(sink_cols, axis=1)
                    sc = lax.dot_general(k2[slot, h, bi, r0:r0 + band, :], qm, (((1,), (1,)), ((), ())),
                                         preferred_element_type=F32)
                    if masked:
                        sc = jnp.where(valid, sc, MASK_VALUE)
                    m_col = jnp.maximum(jnp.max(sc, axis=0, keepdims=True), sink)
                    e = jnp.exp(sc - m_col)
                    inv_denom = 1.0 / (jnp.sum(e, axis=0, keepdims=True) + jnp.exp(sink - m_col))
                    e = e.astype(BF16)
                    yield
                    o_t = lax.dot_general(v2[slot, h, bi, r0:r0 + band, :], e,
                                          (((0,), (0,)), ((), ())), preferred_element_type=F32)
                    o = (o_t * inv_denom).T
                    for jp in range(gqa // 2):
                        o_even = o[(2 * jp) * CHUNK:(2 * jp + 1) * CHUNK, :]
                        o_odd = o[(2 * jp + 1) * CHUNK:(2 * jp + 2) * CHUNK, :]
                        y_blocks[(h * gqa) // 2 + jp] = jnp.where(low_half, o_even, o_odd)
                y_attn = jnp.concatenate(y_blocks, axis=1)
                ya[bi, r0:r0 + CHUNK, :] = _rms_norm_rows(
                    y_attn, gm_ref[:, d_conv:d_conv + d_attn]).astype(BF16)
        r = resid[slot] + _dot(ya[...].reshape(rows, d_attn), wo_ref[d_conv:d_conv + d_attn, :])
        o_ref[...] = _layer_norm_rows(r, g_ref[...], b_ref[...]).reshape(o_ref.shape)

    def run(*stages):
        live = list(stages)
        while live:
            for g in list(live):
                if next(g, StopIteration) is StopIteration:
                    live.remove(g)

    @pl.when(s == 0)
    def _():
        convert()
        run(stage_a(0, 0))

    for parity in range(2):
        @pl.when(jnp.logical_and(jnp.logical_and(s > 0, s < n_tiles), s % 2 == parity))
        def _(parity=parity):
            convert()
            run(stage_b(1 - parity, (s - 1) % n_t), stage_a(parity, s % n_t))

    @pl.when(s == n_tiles)
    def _():
        convert()
        run(stage_b((n_tiles - 1) % 2, (n_tiles - 1) % n_t))


def _mix(x, w_in, w_out, conv_w, sinks, norm_g, ln_g, ln_b, cos, sin, cache_k, cache_v, cache_conv,
         *, nb, tq, mask_halo, alpha, convert=()):
    bsz, t_len, d = x.shape
    d_conv = conv_w.shape[1]
    d_mix = norm_g.shape[1]
    d_attn = d_mix - d_conv
    d_kv = N_KV_HEADS * HEAD_DIM
    assert bsz % nb == 0 and t_len % tq == 0 and tq % CHUNK == 0
    assert d_kv == LANES and d_attn % LANES == 0 and (d_attn // HEAD_DIM) % (2 * N_KV_HEADS) == 0
    assert w_in.shape[1] == 3 * d_conv + d_attn + 2 * d_kv
    n_t = t_len // tq
    assert n_t == 1 or tq >= WINDOW
    n_tiles = (bsz // nb) * n_t
    rows = nb * tq
    ext = WINDOW + tq
    heights = _convert_heights(convert, n_tiles + 1)
    cv_specs, cv_shapes, cv_vmem = _convert_specs(convert, heights, lambda s: s)
    vmem = (w_in.size * BF16_BYTES + w_out.size * BF16_BYTES
            + 2 * 2 * rows * d * F32_BYTES
            + 2 * 2 * tq * LANES * F32_BYTES
            + 2 * 2 * nb * ext * d_kv * F32_BYTES
            + 2 * 2 * N_KV_HEADS * nb * ext * LANES * BF16_BYTES
            + nb * (SUBLANES + tq) * d_conv * F32_BYTES
            + 3 * rows * d_attn * BF16_BYTES
            + rows * d_conv * BF16_BYTES
            + 2 * rows * d * F32_BYTES
            + 4 * 2 * nb * WINDOW * d_kv * F32_BYTES
            + 4 * rows * d_conv * F32_BYTES
            + 2 * rows * d * F32_BYTES
            + cv_vmem)
    kernel = functools.partial(_mix_kernel, nb=nb, tq=tq, n_t=n_t, n_tiles=n_tiles, d_conv=d_conv,
                               d_attn=d_attn, mask_halo=mask_halo, alpha=alpha, n_cv=len(convert))
    tile_a = lambda s: jnp.minimum(s, n_tiles - 1)
    tile_b = lambda s: jnp.maximum(s - 1, 0)
    const = lambda s: (0, 0)
    seq_a = lambda s: (tile_a(s) // n_t, 0, 0)
    return pl.pallas_call(
        kernel,
        grid=(n_tiles + 1,),
        in_specs=[
            pl.BlockSpec(memory_space=pltpu.SMEM),
            pl.BlockSpec((nb, tq, d), lambda s: (tile_a(s) // n_t, tile_a(s) % n_t, 0)),
            pl.BlockSpec(w_in.shape, const, pipeline_mode=pl.Buffered(1)),
            pl.BlockSpec(w_out.shape, const, pipeline_mode=pl.Buffered(1)),
            pl.BlockSpec(conv_w.shape, const),
            pl.BlockSpec(norm_g.shape, const),
            pl.BlockSpec(ln_g.shape, const),
            pl.BlockSpec(ln_b.shape, const),
            pl.BlockSpec((tq, LANES), lambda s: (tile_a(s) % n_t, 0)),
            pl.BlockSpec((tq, LANES), lambda s: (tile_a(s) % n_t, 0)),
            pl.BlockSpec((nb, WINDOW, d_kv), seq_a),
            pl.BlockSpec((nb, WINDOW, d_kv), seq_a),
            pl.BlockSpec((nb, CONV_WIDTH - 1, d_conv), seq_a),
            *cv_specs,
        ],
        out_specs=[
            pl.BlockSpec((nb, tq, d), lambda s: (tile_b(s) // n_t, tile_b(s) % n_t, 0)),
            pl.BlockSpec((nb, CONV_WIDTH - 1, d_conv), seq_a),
            pl.BlockSpec((nb, WINDOW, d_kv), seq_a),
            pl.BlockSpec((nb, WINDOW, d_kv), seq_a),
            *cv_specs,
        ],
        out_shape=[
            jax.ShapeDtypeStruct((bsz, t_len, d), F32),
            jax.ShapeDtypeStruct((bsz, CONV_WIDTH - 1, d_conv), F32),
            jax.ShapeDtypeStruct((bsz, WINDOW, d_kv), F32),
            jax.ShapeDtypeStruct((bsz, WINDOW, d_kv), F32),
            *cv_shapes,
        ],
        scratch_shapes=[
            pltpu.VMEM((2, nb, ext, d_kv), F32),
            pltpu.VMEM((2, nb, ext, d_kv), F32),
            pltpu.VMEM((2, N_KV_HEADS, nb, ext, LANES), BF16),
            pltpu.VMEM((2, N_KV_HEADS, nb, ext, LANES), BF16),
            pltpu.VMEM((nb, SUBLANES + tq, d_conv), F32),
            pltpu.VMEM((2, nb, tq, d_attn), BF16),
            pltpu.VMEM((2, rows, d), F32),
            pltpu.VMEM((nb, tq, d_attn), BF16),
            pltpu.VMEM((rows, d_conv), BF16),
        ],
        compiler_params=pltpu.CompilerParams(
            dimension_semantics=("arbitrary",),
            vmem_limit_bytes=_vmem_limit(vmem)),
        name="mix",
    )(sinks, x, w_in, w_out, conv_w, norm_g, ln_g, ln_b, cos, sin, cache_k, cache_v, cache_conv, *convert)


def _rope_tables(pos):
    half = HEAD_DIM // 2
    inv = ROPE_THETA ** (-jnp.arange(half, dtype=F32) / half)
    ang = pos.astype(F32)[:, None] * inv[None, :]
    cos = jnp.cos(ang)
    sin = jnp.sin(ang)
    reps = LANES // HEAD_DIM
    return jnp.tile(jnp.concatenate([cos, cos], axis=1), (1, reps)), jnp.tile(jnp.concatenate([-sin, sin], axis=1), (1, reps))


def _largest_tile(n, target):
    t = min(n, target)
    while n % t:
        t //= 2
    return t


def _plan(bp, sp, bs, ts):
    ffn_rows = 1024
    ffn_chunk = 512
    ffn_chunk_cast = 256
    mix_rows = 256
    return dict(
        tm_p=_largest_tile(bp * sp, ffn_rows), tm_s=_largest_tile(bs * ts, ffn_rows), tf=ffn_chunk,
        tf_cast=ffn_chunk_cast,
        mix_p=dict(nb=1, tq=_largest_tile(sp, mix_rows)),
        mix_s=dict(nb=_largest_tile(bs, mix_rows // ts), tq=ts))


def kernel(x_prompt, x_sample, cache_conv, cache_k, cache_v, ffn1_w_in, ffn1_w_out, ln1_g, ln1_b, w_mix_in, conv_w, attn_sinks, mix_norm_g, w_mix_out, ln2_g, ln2_b, ffn2_w_in, ffn2_w_out, ln3_g, ln3_b):
    depth = ffn1_w_in.shape[0]
    bp, sp, d = x_prompt.shape
    bs, ts, _ = x_sample.shape
    assert ts == CHUNK
    alpha = (2.0 * depth) ** 0.25
    d_conv = conv_w.shape[-1]
    d_kv = N_KV_HEADS * HEAD_DIM
    plan = _plan(bp, sp, bs, ts)
    tf = _largest_tile(ffn1_w_out.shape[1], plan["tf"])
    tf_cast = _largest_tile(ffn1_w_out.shape[1], plan["tf_cast"])

    cos_p, sin_p = _rope_tables(jnp.arange(sp))
    cos_s, sin_s = _rope_tables(PAST_LEN + jnp.arange(ts))
    zeros_conv = jnp.zeros((bp, CONV_WIDTH - 1, d_conv), F32)
    zeros_kv = jnp.zeros((bp, WINDOW, d_kv), F32)

    xp = x_prompt.reshape(bp * sp, d)
    xs = x_sample.reshape(bs * ts, d)
    outs = {k: [] for k in ("conv_p", "k_p", "v_p", "conv_s", "k_s", "v_s")}
    for l in range(depth):
        g1, b1 = ln1_g[l][None, :], ln1_b[l][None, :]
        g2, b2 = ln2_g[l][None, :], ln2_b[l][None, :]
        g3, b3 = ln3_g[l][None, :], ln3_b[l][None, :]
        gm = mix_norm_g[l][None, :]

        xs, w1_gate, w1_up, w1_out = _ffn(xs, ffn1_w_in[l], ffn1_w_in[l], ffn1_w_out[l], g1, b1, alpha=alpha,
                                          tm=plan["tm_s"], tf=tf_cast, cast_weights=True)
        xp, wm_in, wm_out = _ffn(xp, w1_gate, w1_up, w1_out, g1, b1, alpha=alpha, tm=plan["tm_p"], tf=tf,
                                 convert=(w_mix_in[l], w_mix_out[l]))

        xp, conv_p, k_p, v_p, w2_in, w2_out = _mix(
            xp.reshape(bp, sp, d), wm_in, wm_out, conv_w[l], attn_sinks[l], gm, g2, b2, cos_p, sin_p,
            zeros_kv, zeros_kv, zeros_conv, mask_halo=True, alpha=alpha, **plan["mix_p"],
            convert=(ffn2_w_in[l], ffn2_w_out[l]))
        xs, conv_s, k_s, v_s = _mix(
            xs.reshape(bs, ts, d), wm_in, wm_out, conv_w[l], attn_sinks[l], gm, g2, b2, cos_s, sin_s,
            cache_k[l].reshape(bs, WINDOW, d_kv), cache_v[l].reshape(bs, WINDOW, d_kv), cache_conv[l],
            mask_halo=False, alpha=alpha, **plan["mix_s"])
        xp = xp.reshape(bp * sp, d)
        xs = xs.reshape(bs * ts, d)

        xp, = _ffn(xp, w2_in, w2_in, w2_out, g3, b3, alpha=alpha, tm=plan["tm_p"], tf=tf)
        xs, = _ffn(xs, w2_in, w2_in, w2_out, g3, b3, alpha=alpha, tm=plan["tm_s"], tf=tf)

        kv_shape_p = (bp, WINDOW, N_KV_HEADS, HEAD_DIM)
        kv_shape_s = (bs, WINDOW, N_KV_HEADS, HEAD_DIM)
        outs["conv_p"].append(conv_p)
        outs["k_p"].append(k_p.reshape(kv_shape_p))
        outs["v_p"].append(v_p.reshape(kv_shape_p))
        outs["conv_s"].append(conv_s)
        outs["k_s"].append(k_s.reshape(kv_shape_s))
        outs["v_s"].append(v_s.reshape(kv_shape_s))

    return (xp.reshape(bp, sp, d), xs.reshape(bs, ts, d),
            jnp.stack(outs["conv_p"]), jnp.stack(outs["k_p"]), jnp.stack(outs["v_p"]),
            jnp.stack(outs["conv_s"]), jnp.stack(outs["k_s"]), jnp.stack(outs["v_s"]))
```

```python
import functools

import jax
import jax.numpy as jnp
from jax import lax
from jax.experimental import pallas as pl
from jax.experimental.pallas import tpu as pltpu

CHUNK = 64
HEAD_DIM = 64
N_KV_HEADS = 2
WINDOW = 128
CONV_WIDTH = 3
PAST_LEN = 2048
ROPE_THETA = 10000.0
LN_EPS = 1e-5
RMS_EPS = 1e-6
ATTN_SCALE = HEAD_DIM ** -0.5
MASK_VALUE = -1e30

LANES = 128
SUBLANES = 8
V7X_VMEM_BYTES = 64 * 1024 * 1024
V7X_VMEM_REQUEST_CAP = V7X_VMEM_BYTES - 6 * 1024 * 1024

F32 = jnp.float32
BF16 = jnp.bfloat16
F32_BYTES = 4
BF16_BYTES = 2
BF16_ROW_TILE = 2 * SUBLANES


def _dot(a, b):
    return jnp.dot(a, b, preferred_element_type=F32)


def _layer_norm_rows(y, g, b, scale=1.0):
    mu = jnp.mean(y, axis=-1, keepdims=True)
    d = y - mu
    var = jnp.mean(d * d, axis=-1, keepdims=True)
    return d * lax.rsqrt(var + (scale * scale) * LN_EPS) * g + b


def _rms_norm_rows(y, g):
    inv = lax.rsqrt(jnp.mean(y * y, axis=-1, keepdims=True) + RMS_EPS)
    return y * inv * g


def _vmem_limit(n_bytes):
    assert n_bytes <= V7X_VMEM_REQUEST_CAP, n_bytes
    return int(n_bytes)


CONVERT_BLOCK_BYTES = 1 << 19


def _convert_heights(entries, n_steps):
    heights = []
    for a, _ in entries:
        r, c = a.shape
        ok = [h for h in range(BF16_ROW_TILE, r + 1, BF16_ROW_TILE) if r % h == 0 and r // h <= n_steps]
        heights.append(next((h for h in ok if h * c * F32_BYTES >= CONVERT_BLOCK_BYTES), ok[-1]))
    return heights


def _convert_specs(entries, heights, step_of):
    in_specs, out_specs, shapes, vmem = [], [], [], 0
    for (a, chunk), h in zip(entries, heights):
        r, c = a.shape
        blk = lambda *idx, last=r // h - 1: jnp.minimum(step_of(*idx), last)
        in_specs.append(pl.BlockSpec((h, c), lambda *idx, blk=blk: (blk(*idx), 0)))
        if chunk is None:
            out_specs.append(pl.BlockSpec((h, c), lambda *idx, blk=blk: (blk(*idx), 0)))
            shapes.append(jax.ShapeDtypeStruct((r, c), BF16))
        else:
            nf = c // (2 * chunk)
            out_specs.append(pl.BlockSpec((nf, 2, h, chunk), lambda *idx, blk=blk: (0, 0, blk(*idx), 0)))
            shapes.append(jax.ShapeDtypeStruct((nf, 2, r, chunk), BF16))
        vmem += 2 * h * c * (F32_BYTES + BF16_BYTES)
    return in_specs, out_specs, shapes, vmem


def _convert_blocks(src_refs, dst_refs):
    for src, dst in zip(src_refs, dst_refs):
        if len(dst.shape) == 2:
            dst[...] = src[...].astype(BF16)
        else:
            nf, _, _, chunk = dst.shape
            for half in range(2):
                for c in range(nf):
                    lo = (half * nf + c) * chunk
                    dst[c, half] = src[:, lo:lo + chunk].astype(BF16)


def _ffn_kernel(x_hbm, *refs, alpha, nf, tm, n_cv, cast_weights):
    if cast_weights:
        (wg_ref, wu_ref, wo_ref), refs = refs[:3], refs[3:]
    else:
        (wgu_ref, wo_ref), refs = refs[:2], refs[2:]
    (g_ref, b_ref), refs = refs[:2], refs[2:]
    cv_src, o_ref, cv_dst = refs[:n_cv], refs[n_cv], refs[n_cv + 1:2 * n_cv + 1]
    refs = refs[2 * n_cv + 1:]
    if cast_weights:
        (wgu_out, wo_out), refs = refs[:2], refs[2:]
    xf_ref, xb_ref, act_ref, x_sem = refs[:4]
    w_bufs = refs[4:]
    i = pl.program_id(0)
    j = pl.program_id(1)
    convert = functools.partial(_convert_blocks, cv_src, cv_dst)

    def x_copy(tile):
        return pltpu.make_async_copy(x_hbm.at[pl.ds(tile * tm, tm), :], xf_ref, x_sem)

    def weight(k):
        if not cast_weights:
            return wo_ref[...] if k == 2 else wgu_ref[0, k]
        w_bufs[k][...] = (wg_ref, wu_ref, wo_ref)[k][...].astype(BF16)
        return w_bufs[k][...]

    def gate_up(slot):
        xb = xb_ref[...]
        gate = _dot(xb, weight(0))
        up = _dot(xb, weight(1))
        act_ref[slot] = (gate / (1.0 + jnp.exp(-gate)) * up).astype(BF16)

    def down(slot, w, rows=slice(None)):
        o_ref[rows, :] += _dot(act_ref[slot, rows, :], w)

    @pl.when(jnp.logical_and(i == 0, j == 0))
    def _():
        x_copy(0).start()

    @pl.when(j == 0)
    def _():
        convert()
        x_copy(i).wait()
        x = xf_ref[...]
        xb_ref[...] = x.astype(BF16)
        o_ref[...] = (2.0 * alpha) * x
        gate_up(0)

    @pl.when(jnp.logical_and(j == 1, i + 1 < pl.num_programs(0)))
    def _():
        x_copy(i + 1).start()

    for parity in range(2):
        @pl.when(jnp.logical_and(jnp.logical_and(j > 0, j < nf), j % 2 == parity))
        def _(parity=parity):
            convert()
            gate_up(parity)
            down(1 - parity, weight(2))

    @pl.when(j == nf)
    def _():
        convert()
        w = weight(2)
        for r in range(2):
            rows = slice(r * (tm // 2), (r + 1) * (tm // 2))
            down((nf - 1) % 2, w, rows)
            o_ref[rows, :] = _layer_norm_rows(o_ref[rows, :], g_ref[...], b_ref[...], scale=2.0)

    if cast_weights:
        @pl.when(i == 0)
        def _():
            wgu_out[0, 0] = w_bufs[0][...]
            wgu_out[0, 1] = w_bufs[1][...]

        @pl.when(jnp.logical_and(i == 0, j > 0))
        def _():
            wo_out[...] = w_bufs[2][...]


def _ffn(x, w_in, w_out, g, b, *, alpha, tm, tf, convert=(), cast_to_chunk=None):
    cast_weights = cast_to_chunk is not None
    m, d = x.shape
    f = w_out.shape[0]
    assert m % tm == 0 and f % tf == 0 and tm % (2 * SUBLANES) == 0
    nf = f // tf
    heights = _convert_heights(convert, (m // tm) * (nf + 1))
    cv_in, cv_out, cv_shapes, cv_vmem = _convert_specs(convert, heights, lambda i, j: i * (nf + 1) + j)
    w_bytes = F32_BYTES + BF16_BYTES if cast_weights else BF16_BYTES
    vmem = ((3 * d * tf * BF16_BYTES if cast_weights else 0)
            + tm * d * F32_BYTES
            + 2 * tm * d * F32_BYTES
            + tm * d * BF16_BYTES
            + 2 * tm * tf * BF16_BYTES
            + 2 * 3 * d * tf * w_bytes
            + 2 * tm * tf * F32_BYTES
            + (tm // 2) * d * F32_BYTES
            + cv_vmem)
    chunk_in = lambda i, j: jnp.minimum(j, nf - 1)
    chunk_out = lambda i, j: jnp.maximum(j - 1, 0)
    down_spec = pl.BlockSpec((tf, d), lambda i, j: (chunk_out(i, j), 0))
    w_out_specs, w_out_shapes, w_scratch = [], [], []
    if cast_weights:
        assert cast_to_chunk % tf == 0 and w_in.shape == (d, 2 * f)
        per = cast_to_chunk // tf
        w_specs = [pl.BlockSpec((d, tf), lambda i, j: (0, chunk_in(i, j))),
                   pl.BlockSpec((d, tf), lambda i, j: (0, chunk_in(i, j) + nf)), down_spec]
        w_args = (w_in, w_in, w_out)
        done_in = lambda i, j: jnp.where(i == 0, chunk_in(i, j), nf - 1)
        w_out_specs = [
            pl.BlockSpec((1, 2, d, tf), lambda i, j: (done_in(i, j) // per, 0, 0, done_in(i, j) % per)),
            pl.BlockSpec((tf, d), lambda i, j: (jnp.where(i == 0, chunk_out(i, j), nf - 1), 0))]
        w_out_shapes = [jax.ShapeDtypeStruct((f // cast_to_chunk, 2, d, cast_to_chunk), BF16),
                        jax.ShapeDtypeStruct(w_out.shape, BF16)]
        w_scratch = [pltpu.VMEM((d, tf), BF16), pltpu.VMEM((d, tf), BF16), pltpu.VMEM((tf, d), BF16)]
    else:
        assert w_in.shape == (nf, 2, d, tf)
        w_specs = [pl.BlockSpec((1, 2, d, tf), lambda i, j: (chunk_in(i, j), 0, 0, 0)), down_spec]
        w_args = (w_in, w_out)
    return pl.pallas_call(
        functools.partial(_ffn_kernel, alpha=alpha, nf=nf, tm=tm, n_cv=len(convert), cast_weights=cast_weights),
        grid=(m // tm, nf + 1),
        in_specs=[
            pl.BlockSpec(memory_space=pl.ANY),
            *w_specs,
            pl.BlockSpec((1, d), lambda i, j: (0, 0)),
            pl.BlockSpec((1, d), lambda i, j: (0, 0)),
            *cv_in,
        ],
        out_specs=[pl.BlockSpec((tm, d), lambda i, j: (i, 0)), *cv_out, *w_out_specs],
        out_shape=[jax.ShapeDtypeStruct((m, d), F32), *cv_shapes, *w_out_shapes],
        scratch_shapes=[pltpu.VMEM((tm, d), F32), pltpu.VMEM((tm, d), BF16), pltpu.VMEM((2, tm, tf), BF16),
                        pltpu.SemaphoreType.DMA, *w_scratch],
        compiler_params=pltpu.CompilerParams(
            dimension_semantics=("arbitrary", "arbitrary"),
            vmem_limit_bytes=_vmem_limit(vmem)),
        name="ffn",
    )(x, *w_args, g, b, *[a for a, _ in convert])


def _rotate_half(x):
    lane = lax.broadcasted_iota(jnp.int32, x.shape, 1)
    first_half = (lane % HEAD_DIM) < (HEAD_DIM // 2)
    return jnp.where(first_half, pltpu.roll(x, LANES - HEAD_DIM // 2, 1), pltpu.roll(x, HEAD_DIM // 2, 1))


def _mix_kernel(sink_ref, x_ref, w_ref, wo_ref, cw_ref, gm_ref, g_ref, b_ref, cos_ref, sin_ref,
                ck_ref, cv_ref, cc_ref, *rest,
                nb, tq, n_t, n_tiles, d_conv, d_attn, mask_halo, alpha, n_cv):
    cv_src, (o_ref, nconv_ref, nk_ref, nv_ref) = rest[:n_cv], rest[n_cv:n_cv + 4]
    cv_dst = rest[n_cv + 4:2 * n_cv + 4]
    kext, vext, k2, v2, uext, qs, resid, ya, yc = rest[2 * n_cv + 4:]
    s = pl.program_id(0)
    convert = functools.partial(_convert_blocks, cv_src, cv_dst)
    d_kv = N_KV_HEADS * HEAD_DIM
    gqa = d_attn // HEAD_DIM // N_KV_HEADS
    pad = SUBLANES
    tail = CONV_WIDTH - 1
    rows = nb * tq
    band = WINDOW + CHUNK
    ext = WINDOW + tq
    n_chunks = tq // CHUNK
    n_pairs = d_attn // LANES
    q_off = 3 * d_conv

    def stage_a(slot, t):
        first = t == 0
        x = x_ref[...].reshape(rows, x_ref.shape[-1])
        xb = x.astype(BF16)
        half = d_conv // 2
        col = lambda g, hf: slice(g * d_conv + hf * half, g * d_conv + (hf + 1) * half)

        b_gate = []
        for hf in range(2):
            b_gate.append(_dot(xb, w_ref[:, col(0, hf)]))
            yield
        c_gate = []
        for hf in range(2):
            c_gate.append(_dot(xb, w_ref[:, col(1, hf)]))
            yield
        y_half = []
        for hf in range(2):
            ch = slice(hf * half, (hf + 1) * half)
            hc = _dot(xb, w_ref[:, col(2, hf)])
            u = (c_gate[hf] * hc).reshape(nb, tq, half)
            uext[:, pad - tail:pad, ch] = jnp.where(first, cc_ref[:, :, ch], uext[:, pad + tq - tail:pad + tq, ch])
            uext[:, pad:pad + tq, ch] = u
            z = cw_ref[CONV_WIDTH - 1:CONV_WIDTH, ch] * u
            for jj in range(tail):
                off = pad - tail + jj
                z = z + cw_ref[jj:jj + 1, ch] * uext[:, off:off + tq, ch]
            y_half.append(b_gate[hf].reshape(nb, tq, half) * z)
            if hf == 0:
                yield
        nconv_ref[...] = uext[:, pad + tq - tail:pad + tq, :]
        ssq = sum(jnp.sum(y * y, axis=-1, keepdims=True) for y in y_half)
        inv = lax.rsqrt(ssq * (1.0 / d_conv) + RMS_EPS)
        for hf in range(2):
            ch = slice(hf * half, (hf + 1) * half)
            yc[:, ch] = (y_half[hf] * inv * gm_ref[:, ch]).reshape(rows, half).astype(BF16)
        yield

        row0 = t * tq if isinstance(t, int) else pl.multiple_of(t * tq, tq)
        t_rows = pl.ds(row0, tq)
        cos = cos_ref[t_rows, :]
        sin = sin_ref[t_rows, :]
        kext[slot, :, 0:WINDOW, :] = jnp.where(first, ck_ref[...], kext[1 - slot, :, tq:ext, :])
        vext[slot, :, 0:WINDOW, :] = jnp.where(first, cv_ref[...], vext[1 - slot, :, tq:ext, :])
        kv = _dot(xb, w_ref[:, q_off + d_attn:q_off + d_attn + 2 * d_kv]).reshape(nb, tq, 2 * d_kv)
        for bi in range(nb):
            k_b = kv[bi, :, 0:d_kv]
            kext[slot, bi, WINDOW:ext, :] = k_b * cos + _rotate_half(k_b) * sin
            vext[slot, bi, WINDOW:ext, :] = kv[bi, :, d_kv:2 * d_kv]
        nk_ref[...] = kext[slot, :, tq:ext, :]
        nv_ref[...] = vext[slot, :, tq:ext, :]

        low_half_ext = lax.broadcasted_iota(jnp.int32, (ext, LANES), 1) < HEAD_DIM
        for bi in range(nb):
            for src, dst in ((kext, k2), (vext, v2)):
                rows_f = src[slot, bi]
                rows_sw = pltpu.roll(rows_f, HEAD_DIM, 1)
                dst[slot, 0, bi] = jnp.where(low_half_ext, rows_f, rows_sw).astype(BF16)
                dst[slot, 1, bi] = jnp.where(low_half_ext, rows_sw, rows_f).astype(BF16)
        yield

        cos_q = cos * ATTN_SCALE
        sin_q = sin * ATTN_SCALE
        for hf in range(2):
            lo = hf * (d_attn // 2)
            q = _dot(xb, w_ref[:, q_off + lo:q_off + lo + d_attn // 2]).reshape(nb, tq, d_attn // 2)
            for bi in range(nb):
                for p in range(n_pairs // 2):
                    qp = q[bi, :, p * LANES:(p + 1) * LANES]
                    qs[slot, bi, :, lo + p * LANES:lo + (p + 1) * LANES] = (
                        qp * cos_q + _rotate_half(qp) * sin_q).astype(BF16)
            yield

        resid[slot] = alpha * x + _dot(yc[...], wo_ref[0:d_conv, :])

    def stage_b(slot, t):
        low_half = lax.broadcasted_iota(jnp.int32, (CHUNK, LANES), 1) < HEAD_DIM
        low_half_row = lax.broadcasted_iota(jnp.int32, (1, LANES), 1) < HEAD_DIM
        zero_q = jnp.zeros((CHUNK, LANES), BF16)
        for bi in range(nb):
            for c in range(n_chunks):
                r0 = c * CHUNK
                masked = mask_halo and c < WINDOW // CHUNK
                if masked:
                    key_pos = (t * n_chunks + (c - WINDOW // CHUNK)) * CHUNK + lax.broadcasted_iota(
                        jnp.int32, (band, 1), 0)
                    valid = key_pos >= 0
                y_blocks = [None] * n_pairs
                for h in range(N_KV_HEADS):
                    q_stack = []
                    sink_cols = []
                    for gi in range(gqa):
                        n = h * gqa + gi
                        qp = qs[slot, bi, r0:r0 + CHUNK, (n // 2) * LANES:(n // 2 + 1) * LANES]
                        keep = low_half if n % 2 == 0 else jnp.logical_not(low_half)
                        q_stack.append(jnp.where(keep, qp, zero_q))
                        if gi % 2 == 0:
                            sink_cols.append(jnp.where(low_half_row, sink_ref[n], sink_ref[n + 1]))
                    qm = jnp.concatenate(q_stack, axis=0)
                    sink = jnp.concatenate(sink_cols, axis=1)
                    sc = lax.dot_general(k2[slot, h, bi, r0:r0 + band, :], qm, (((1,), (1,)), ((), ())),
                                         preferred_element_type=F32)
                    if masked:
                        sc = jnp.where(valid, sc, MASK_VALUE)
                    m_col = jnp.maximum(jnp.max(sc, axis=0, keepdims=True), sink)
                    e = jnp.exp(sc - m_col)
                    inv_denom = 1.0 / (jnp.sum(e, axis=0, keepdims=True) + jnp.exp(sink - m_col))
                    e = e.astype(BF16)
                    yield
                    o_t = lax.dot_general(v2[slot, h, bi, r0:r0 + band, :], e,
                                          (((0,), (0,)), ((), ())), preferred_element_type=F32)
                    o = (o_t * inv_denom).T
                    for jp in range(gqa // 2):
                        o_even = o[(2 * jp) * CHUNK:(2 * jp + 1) * CHUNK, :]
                        o_odd = o[(2 * jp + 1) * CHUNK:(2 * jp + 2) * CHUNK, :]
                        y_blocks[(h * gqa) // 2 + jp] = jnp.where(low_half, o_even, o_odd)
                y_attn = jnp.concatenate(y_blocks, axis=1)
                ya[bi, r0:r0 + CHUNK, :] = _rms_norm_rows(
                    y_attn, gm_ref[:, d_conv:d_conv + d_attn]).astype(BF16)
        r = resid[slot] + _dot(ya[...].reshape(rows, d_attn), wo_ref[d_conv:d_conv + d_attn, :])
        o_ref[...] = _layer_norm_rows(r, g_ref[...], b_ref[...]).reshape(o_ref.shape)

    def run(*stages):
        live = list(stages)
        while live:
            for g in list(live):
                if next(g, StopIteration) is StopIteration:
                    live.remove(g)

    @pl.when(s == 0)
    def _():
        convert()
        run(stage_a(0, 0))

    for parity in range(2):
        @pl.when(jnp.logical_and(jnp.logical_and(s > 0, s < n_tiles), s % 2 == parity))
        def _(parity=parity):
            convert()
            run(stage_b(1 - parity, (s - 1) % n_t), stage_a(parity, s % n_t))

    @pl.when(s == n_tiles)
    def _():
        convert()
        run(stage_b((n_tiles - 1) % 2, (n_tiles - 1) % n_t))


def _mix(x, w_in, w_out, conv_w, sinks, norm_g, ln_g, ln_b, cos, sin, cache_k, cache_v, cache_conv,
         *, nb, tq, mask_halo, alpha, convert=()):
    bsz, t_len, d = x.shape
    d_conv = conv_w.shape[1]
    d_mix = norm_g.shape[1]
    d_attn = d_mix - d_conv
    d_kv = N_KV_HEADS * HEAD_DIM
    assert bsz % nb == 0 and t_len % tq == 0 and tq % CHUNK == 0
    assert d_kv == LANES and d_attn % LANES == 0 and (d_attn // HEAD_DIM) % (2 * N_KV_HEADS) == 0
    assert w_in.shape[1] == 3 * d_conv + d_attn + 2 * d_kv
    n_t = t_len // tq
    assert n_t == 1 or tq >= WINDOW
    n_tiles = (bsz // nb) * n_t
    rows = nb * tq
    ext = WINDOW + tq
    heights = _convert_heights(convert, n_tiles + 1)
    cv_in, cv_out, cv_shapes, cv_vmem = _convert_specs(convert, heights, lambda s: s)
    vmem = (w_in.size * BF16_BYTES + w_out.size * BF16_BYTES
            + 2 * 2 * rows * d * F32_BYTES
            + 2 * t_len * LANES * F32_BYTES
            + 2 * 2 * nb * ext * d_kv * F32_BYTES
            + 2 * 2 * N_KV_HEADS * nb * ext * LANES * BF16_BYTES
            + nb * (SUBLANES + tq) * d_conv * F32_BYTES
            + 3 * rows * d_attn * BF16_BYTES
            + rows * d_conv * BF16_BYTES
            + 2 * rows * d * F32_BYTES
            + 4 * 2 * nb * WINDOW * d_kv * F32_BYTES
            + 4 * rows * d_conv * F32_BYTES
            + 2 * rows * d * F32_BYTES
            + cv_vmem)
    kernel = functools.partial(_mix_kernel, nb=nb, tq=tq, n_t=n_t, n_tiles=n_tiles, d_conv=d_conv,
                               d_attn=d_attn, mask_halo=mask_halo, alpha=alpha, n_cv=len(convert))
    tile_a = lambda s: jnp.minimum(s, n_tiles - 1)
    tile_b = lambda s: jnp.maximum(s - 1, 0)
    const = lambda s: (0, 0)
    seq_a = lambda s: (tile_a(s) // n_t, 0, 0)
    return pl.pallas_call(
        kernel,
        grid=(n_tiles + 1,),
        in_specs=[
            pl.BlockSpec(memory_space=pltpu.SMEM),
            pl.BlockSpec((nb, tq, d), lambda s: (tile_a(s) // n_t, tile_a(s) % n_t, 0)),
            pl.BlockSpec(w_in.shape, const, pipeline_mode=pl.Buffered(1)),
            pl.BlockSpec(w_out.shape, const, pipeline_mode=pl.Buffered(1)),
            pl.BlockSpec(conv_w.shape, const),
            pl.BlockSpec(norm_g.shape, const),
            pl.BlockSpec(ln_g.shape, const),
            pl.BlockSpec(ln_b.shape, const),
            pl.BlockSpec(cos.shape, const, pipeline_mode=pl.Buffered(1)),
            pl.BlockSpec(sin.shape, const, pipeline_mode=pl.Buffered(1)),
            pl.BlockSpec((nb, WINDOW, d_kv), seq_a),
            pl.BlockSpec((nb, WINDOW, d_kv), seq_a),
            pl.BlockSpec((nb, CONV_WIDTH - 1, d_conv), seq_a),
            *cv_in,
        ],
        out_specs=[
            pl.BlockSpec((nb, tq, d), lambda s: (tile_b(s) // n_t, tile_b(s) % n_t, 0)),
            pl.BlockSpec((nb, CONV_WIDTH - 1, d_conv), seq_a),
            pl.BlockSpec((nb, WINDOW, d_kv), seq_a),
            pl.BlockSpec((nb, WINDOW, d_kv), seq_a),
            *cv_out,
        ],
        out_shape=[
            jax.ShapeDtypeStruct((bsz, t_len, d), F32),
            jax.ShapeDtypeStruct((bsz, CONV_WIDTH - 1, d_conv), F32),
            jax.ShapeDtypeStruct((bsz, WINDOW, d_kv), F32),
            jax.ShapeDtypeStruct((bsz, WINDOW, d_kv), F32),
            *cv_shapes,
        ],
        scratch_shapes=[
            pltpu.VMEM((2, nb, ext, d_kv), F32),
            pltpu.VMEM((2, nb, ext, d_kv), F32),
            pltpu.VMEM((2, N_KV_HEADS, nb, ext, LANES), BF16),
            pltpu.VMEM((2, N_KV_HEADS, nb, ext, LANES), BF16),
            pltpu.VMEM((nb, SUBLANES + tq, d_conv), F32),
            pltpu.VMEM((2, nb, tq, d_attn), BF16),
            pltpu.VMEM((2, rows, d), F32),
            pltpu.VMEM((nb, tq, d_attn), BF16),
            pltpu.VMEM((rows, d_conv), BF16),
        ],
        compiler_params=pltpu.CompilerParams(
            dimension_semantics=("arbitrary",),
            vmem_limit_bytes=_vmem_limit(vmem)),
        name="mix",
    )(sinks, x, w_in, w_out, conv_w, norm_g, ln_g, ln_b, cos, sin, cache_k, cache_v, cache_conv,
      *[a for a, _ in convert])


def _rope_tables(pos):
    half = HEAD_DIM // 2
    inv = ROPE_THETA ** (-jnp.arange(half, dtype=F32) / half)
    ang = pos.astype(F32)[:, None] * inv[None, :]
    cos = jnp.cos(ang)
    sin = jnp.sin(ang)
    reps = LANES // HEAD_DIM
    return jnp.tile(jnp.concatenate([cos, cos], axis=1), (1, reps)), jnp.tile(jnp.concatenate([-sin, sin], axis=1), (1, reps))


def _largest_tile(n, target):
    t = min(n, target)
    while n % t:
        t //= 2
    return t


def _plan(bp, sp, bs, ts):
    ffn_rows = 1024
    ffn_chunk = 512
    ffn_chunk_cast = 256
    mix_rows = 256
    return dict(
        tm_p=_largest_tile(bp * sp, ffn_rows), tm_s=_largest_tile(bs * ts, ffn_rows), tf=ffn_chunk,
        tf_cast=ffn_chunk_cast,
        mix_p=dict(nb=1, tq=_largest_tile(sp, mix_rows)),
        mix_s=dict(nb=_largest_tile(bs, mix_rows // ts), tq=ts))


def kernel(x_prompt, x_sample, cache_conv, cache_k, cache_v, ffn1_w_in, ffn1_w_out, ln1_g, ln1_b, w_mix_in, conv_w, attn_sinks, mix_norm_g, w_mix_out, ln2_g, ln2_b, ffn2_w_in, ffn2_w_out, ln3_g, ln3_b):
    depth = ffn1_w_in.shape[0]
    bp, sp, d = x_prompt.shape
    bs, ts, _ = x_sample.shape
    assert ts == CHUNK
    alpha = (2.0 * depth) ** 0.25
    d_conv = conv_w.shape[-1]
    d_kv = N_KV_HEADS * HEAD_DIM
    plan = _plan(bp, sp, bs, ts)
    tf = _largest_tile(ffn1_w_out.shape[1], plan["tf"])
    tf_cast = _largest_tile(ffn1_w_out.shape[1], plan["tf_cast"])

    cos_p, sin_p = _rope_tables(jnp.arange(sp))
    cos_s, sin_s = _rope_tables(PAST_LEN + jnp.arange(ts))
    zeros_conv = jnp.zeros((bp, CONV_WIDTH - 1, d_conv), F32)
    zeros_kv = jnp.zeros((bp, WINDOW, d_kv), F32)

    xp = x_prompt.reshape(bp * sp, d)
    xs = x_sample.reshape(bs * ts, d)
    outs = {k: [] for k in ("conv_p", "k_p", "v_p", "conv_s", "k_s", "v_s")}
    for l in range(depth):
        g1, b1 = ln1_g[l][None, :], ln1_b[l][None, :]
        g2, b2 = ln2_g[l][None, :], ln2_b[l][None, :]
        g3, b3 = ln3_g[l][None, :], ln3_b[l][None, :]
        gm = mix_norm_g[l][None, :]

        xs, w1_in, w1_out = _ffn(xs, ffn1_w_in[l], ffn1_w_out[l], g1, b1, alpha=alpha,
                                 tm=plan["tm_s"], tf=tf_cast, cast_to_chunk=tf)
        xp, wm_in, wm_out = _ffn(xp, w1_in, w1_out, g1, b1, alpha=alpha, tm=plan["tm_p"], tf=tf,
                                 convert=((w_mix_in[l], None), (w_mix_out[l], None)))

        xp, conv_p, k_p, v_p, w2_in, w2_out = _mix(
            xp.reshape(bp, sp, d), wm_in, wm_out, conv_w[l], attn_sinks[l], gm, g2, b2, cos_p, sin_p,
            zeros_kv, zeros_kv, zeros_conv, mask_halo=True, alpha=alpha, **plan["mix_p"],
            convert=((ffn2_w_in[l], tf), (ffn2_w_out[l], None)))
        xs, conv_s, k_s, v_s = _mix(
            xs.reshape(bs, ts, d), wm_in, wm_out, conv_w[l], attn_sinks[l], gm, g2, b2, cos_s, sin_s,
            cache_k[l].reshape(bs, WINDOW, d_kv), cache_v[l].reshape(bs, WINDOW, d_kv), cache_conv[l],
            mask_halo=False, alpha=alpha, **plan["mix_s"])
        xp = xp.reshape(bp * sp, d)
        xs = xs.reshape(bs * ts, d)

        xp, = _ffn(xp, w2_in, w2_out, g3, b3, alpha=alpha, tm=plan["tm_p"], tf=tf)
        xs, = _ffn(xs, w2_in, w2_out, g3, b3, alpha=alpha, tm=plan["tm_s"], tf=tf)

        kv_shape_p = (bp, WINDOW, N_KV_HEADS, HEAD_DIM)
        kv_shape_s = (bs, WINDOW, N_KV_HEADS, HEAD_DIM)
        outs["conv_p"].append(conv_p)
        outs["k_p"].append(k_p.reshape(kv_shape_p))
        outs["v_p"].append(v_p.reshape(kv_shape_p))
        outs["conv_s"].append(conv_s)
        outs["k_s"].append(k_s.reshape(kv_shape_s))
        outs["v_s"].append(v_s.reshape(kv_shape_s))

    return (xp.reshape(bp, sp, d), xs.reshape(bs, ts, d),
            jnp.stack(outs["conv_p"]), jnp.stack(outs["k_p"]), jnp.stack(outs["v_p"]),
            jnp.stack(outs["conv_s"]), jnp.stack(outs["k_s"]), jnp.stack(outs["v_s"]))
```

```python
import functools

import jax
import jax.numpy as jnp
from jax import lax
from jax.experimental import pallas as pl
from jax.experimental.pallas import tpu as pltpu

CHUNK = 64
HEAD_DIM = 64
N_KV_HEADS = 2
WINDOW = 128
CONV_WIDTH = 3
PAST_LEN = 2048
ROPE_THETA = 10000.0
LN_EPS = 1e-5
RMS_EPS = 1e-6
ATTN_SCALE = HEAD_DIM ** -0.5
MASK_VALUE = -1e30

LANES = 128
SUBLANES = 8
V7X_VMEM_BYTES = 64 * 1024 * 1024
V7X_VMEM_REQUEST_CAP = V7X_VMEM_BYTES - 6 * 1024 * 1024

F32 = jnp.float32
BF16 = jnp.bfloat16
F32_BYTES = 4
BF16_BYTES = 2
BF16_ROW_TILE = 2 * SUBLANES


def _dot(a, b):
    return jnp.dot(a, b, preferred_element_type=F32)


def _layer_norm_rows(y, g, b, scale=1.0):
    mu = jnp.mean(y, axis=-1, keepdims=True)
    d = y - mu
    var = jnp.mean(d * d, axis=-1, keepdims=True)
    return d * lax.rsqrt(var + (scale * scale) * LN_EPS) * g + b


def _rms_norm_rows(y, g):
    inv = lax.rsqrt(jnp.mean(y * y, axis=-1, keepdims=True) + RMS_EPS)
    return y * inv * g


def _vmem_limit(n_bytes):
    assert n_bytes <= V7X_VMEM_REQUEST_CAP, n_bytes
    return int(n_bytes)


def _convert_heights(arrays, n_steps):
    heights = []
    for a in arrays:
        r = a.shape[0]
        heights.append(next(h for h in range(BF16_ROW_TILE, r + 1, BF16_ROW_TILE)
                            if r % h == 0 and r // h <= n_steps))
    return heights


def _convert_specs(arrays, heights, step_of):
    specs, shapes, vmem = [], [], 0
    for a, h in zip(arrays, heights):
        last = a.shape[0] // h - 1
        specs.append(pl.BlockSpec((h, a.shape[1]), lambda *idx, last=last: (jnp.minimum(step_of(*idx), last), 0)))
        shapes.append(jax.ShapeDtypeStruct(a.shape, BF16))
        vmem += 2 * h * a.shape[1] * (F32_BYTES + BF16_BYTES)
    return specs, shapes, vmem


def _convert_blocks(src_refs, dst_refs):
    for src, dst in zip(src_refs, dst_refs):
        dst[...] = src[...].astype(BF16)


def _ffn_kernel(x_hbm, wg_ref, wu_ref, wo_ref, g_ref, b_ref, *rest, alpha, nf, tm, n_cv, cast_weights):
    cv_src, o_ref, cv_dst = rest[:n_cv], rest[n_cv], rest[n_cv + 1:2 * n_cv + 1]
    rest = rest[2 * n_cv + 1:]
    w_outs = w_bufs = (None, None, None)
    if cast_weights:
        w_outs, rest = rest[:3], rest[3:]
    xf_ref, xb_ref, act_ref, x_sem = rest[:4]
    if cast_weights:
        w_bufs = rest[4:]
    i = pl.program_id(0)
    j = pl.program_id(1)
    convert = functools.partial(_convert_blocks, cv_src, cv_dst)

    def x_copy(tile):
        return pltpu.make_async_copy(x_hbm.at[pl.ds(tile * tm, tm), :], xf_ref, x_sem)

    def weight(ref, k):
        if not cast_weights:
            return ref[...]
        w_bufs[k][...] = ref[...].astype(BF16)
        return w_bufs[k][...]

    def gate_up(slot):
        xb = xb_ref[...]
        gate = _dot(xb, weight(wg_ref, 0))
        up = _dot(xb, weight(wu_ref, 1))
        act_ref[slot] = (gate / (1.0 + jnp.exp(-gate)) * up).astype(BF16)

    def down(slot, w, rows=slice(None)):
        o_ref[rows, :] += _dot(act_ref[slot, rows, :], w)

    @pl.when(jnp.logical_and(i == 0, j == 0))
    def _():
        x_copy(0).start()

    @pl.when(j == 0)
    def _():
        convert()
        x_copy(i).wait()
        x = xf_ref[...]
        xb_ref[...] = x.astype(BF16)
        o_ref[...] = (2.0 * alpha) * x
        gate_up(0)

    @pl.when(jnp.logical_and(j == 1, i + 1 < pl.num_programs(0)))
    def _():
        x_copy(i + 1).start()

    for parity in range(2):
        @pl.when(jnp.logical_and(jnp.logical_and(j > 0, j < nf), j % 2 == parity))
        def _(parity=parity):
            convert()
            gate_up(parity)
            down(1 - parity, weight(wo_ref, 2))

    @pl.when(j == nf)
    def _():
        convert()
        w = weight(wo_ref, 2)
        for r in range(4):
            rows = slice(r * (tm // 4), (r + 1) * (tm // 4))
            down((nf - 1) % 2, w, rows)
            o_ref[rows, :] = _layer_norm_rows(o_ref[rows, :], g_ref[...], b_ref[...], scale=2.0)

    if cast_weights:
        @pl.when(i == 0)
        def _():
            w_outs[0][...] = w_bufs[0][...]
            w_outs[1][...] = w_bufs[1][...]

        @pl.when(jnp.logical_and(i == 0, j > 0))
        def _():
            w_outs[2][...] = w_bufs[2][...]


def _ffn(x, w_gate, w_up, w_out, g, b, *, alpha, tm, tf, convert=(), cast_weights=False):
    m, d = x.shape
    f = w_out.shape[0]
    assert m % tm == 0 and f % tf == 0 and tm % (2 * SUBLANES) == 0
    nf = f // tf
    up_offset = nf if w_up.shape[1] == 2 * f else 0
    heights = _convert_heights(convert, (m // tm) * (nf + 1))
    cv_specs, cv_shapes, cv_vmem = _convert_specs(convert, heights, lambda i, j: i * (nf + 1) + j)
    w_bytes = F32_BYTES + BF16_BYTES if cast_weights else BF16_BYTES
    vmem = ((3 * d * tf * BF16_BYTES if cast_weights else 0)
            + tm * d * F32_BYTES
            + 2 * tm * d * F32_BYTES
            + tm * d * BF16_BYTES
            + 2 * tm * tf * BF16_BYTES
            + 2 * 3 * d * tf * w_bytes
            + 2 * tm * tf * F32_BYTES
            + (tm // 2) * d * F32_BYTES
            + cv_vmem)
    gate_spec = pl.BlockSpec((d, tf), lambda i, j: (0, jnp.minimum(j, nf - 1)))
    up_spec = pl.BlockSpec((d, tf), lambda i, j: (0, jnp.minimum(j, nf - 1) + up_offset))
    down_spec = pl.BlockSpec((tf, d), lambda i, j: (jnp.maximum(j - 1, 0), 0))
    w_specs = [gate_spec, up_spec, down_spec]
    w_out_specs, w_out_shapes, w_scratch = [], [], []
    if cast_weights:
        in_out = pl.BlockSpec((d, tf), lambda i, j: (0, jnp.where(i == 0, jnp.minimum(j, nf - 1), nf - 1)))
        down_out = pl.BlockSpec((tf, d), lambda i, j: (jnp.where(i == 0, jnp.maximum(j - 1, 0), nf - 1), 0))
        w_out_specs = [in_out, in_out, down_out]
        w_out_shapes = [jax.ShapeDtypeStruct((d, f), BF16)] * 2 + [jax.ShapeDtypeStruct(w_out.shape, BF16)]
        w_scratch = [pltpu.VMEM((d, tf), BF16), pltpu.VMEM((d, tf), BF16), pltpu.VMEM((tf, d), BF16)]
    return pl.pallas_call(
        functools.partial(_ffn_kernel, alpha=alpha, nf=nf, tm=tm, n_cv=len(convert), cast_weights=cast_weights),
        grid=(m // tm, nf + 1),
        in_specs=[
            pl.BlockSpec(memory_space=pl.ANY),
            *w_specs,
            pl.BlockSpec((1, d), lambda i, j: (0, 0)),
            pl.BlockSpec((1, d), lambda i, j: (0, 0)),
            *cv_specs,
        ],
        out_specs=[pl.BlockSpec((tm, d), lambda i, j: (i, 0)), *cv_specs, *w_out_specs],
        out_shape=[jax.ShapeDtypeStruct((m, d), F32), *cv_shapes, *w_out_shapes],
        scratch_shapes=[pltpu.VMEM((tm, d), F32), pltpu.VMEM((tm, d), BF16), pltpu.VMEM((2, tm, tf), BF16),
                        pltpu.SemaphoreType.DMA, *w_scratch],
        compiler_params=pltpu.CompilerParams(
            dimension_semantics=("arbitrary", "arbitrary"),
            vmem_limit_bytes=_vmem_limit(vmem)),
        name="ffn",
    )(x, w_gate, w_up, w_out, g, b, *convert)


def _rotate_half(x):
    lane = lax.broadcasted_iota(jnp.int32, x.shape, 1)
    first_half = (lane % HEAD_DIM) < (HEAD_DIM // 2)
    return jnp.where(first_half, pltpu.roll(x, LANES - HEAD_DIM // 2, 1), pltpu.roll(x, HEAD_DIM // 2, 1))


def _mix_kernel(sink_ref, x_ref, w_ref, wo_ref, cw_ref, gm_ref, g_ref, b_ref, cos_ref, sin_ref,
                ck_ref, cv_ref, cc_ref, *rest,
                nb, tq, n_t, n_tiles, d_conv, d_attn, mask_halo, alpha, n_cv):
    cv_src, (o_ref, nconv_ref, nk_ref, nv_ref) = rest[:n_cv], rest[n_cv:n_cv + 4]
    cv_dst = rest[n_cv + 4:2 * n_cv + 4]
    kext, vext, k2, v2, uext, qs, resid, ya, yc = rest[2 * n_cv + 4:]
    s = pl.program_id(0)
    convert = functools.partial(_convert_blocks, cv_src, cv_dst)
    d_kv = N_KV_HEADS * HEAD_DIM
    gqa = d_attn // HEAD_DIM // N_KV_HEADS
    pad = SUBLANES
    tail = CONV_WIDTH - 1
    rows = nb * tq
    band = WINDOW + CHUNK
    ext = WINDOW + tq
    n_chunks = tq // CHUNK
    n_pairs = d_attn // LANES
    q_off = 3 * d_conv

    def stage_a(slot, t):
        first = t == 0
        x = x_ref[...].reshape(rows, x_ref.shape[-1])
        xb = x.astype(BF16)
        half = d_conv // 2
        col = lambda g, hf: slice(g * d_conv + hf * half, g * d_conv + (hf + 1) * half)

        b_gate = []
        for hf in range(2):
            b_gate.append(_dot(xb, w_ref[:, col(0, hf)]))
            yield
        c_gate = []
        for hf in range(2):
            c_gate.append(_dot(xb, w_ref[:, col(1, hf)]))
            yield
        y_half = []
        for hf in range(2):
            ch = slice(hf * half, (hf + 1) * half)
            hc = _dot(xb, w_ref[:, col(2, hf)])
            u = (c_gate[hf] * hc).reshape(nb, tq, half)
            uext[:, pad - tail:pad, ch] = jnp.where(first, cc_ref[:, :, ch], uext[:, pad + tq - tail:pad + tq, ch])
            uext[:, pad:pad + tq, ch] = u
            z = cw_ref[CONV_WIDTH - 1:CONV_WIDTH, ch] * u
            for jj in range(tail):
                off = pad - tail + jj
                z = z + cw_ref[jj:jj + 1, ch] * uext[:, off:off + tq, ch]
            y_half.append(b_gate[hf].reshape(nb, tq, half) * z)
            if hf == 0:
                yield
        nconv_ref[...] = uext[:, pad + tq - tail:pad + tq, :]
        ssq = sum(jnp.sum(y * y, axis=-1, keepdims=True) for y in y_half)
        inv = lax.rsqrt(ssq * (1.0 / d_conv) + RMS_EPS)
        for hf in range(2):
            ch = slice(hf * half, (hf + 1) * half)
            yc[:, ch] = (y_half[hf] * inv * gm_ref[:, ch]).reshape(rows, half).astype(BF16)
        yield

        cos = cos_ref[...]
        sin = sin_ref[...]
        kext[slot, :, 0:WINDOW, :] = jnp.where(first, ck_ref[...], kext[1 - slot, :, tq:ext, :])
        vext[slot, :, 0:WINDOW, :] = jnp.where(first, cv_ref[...], vext[1 - slot, :, tq:ext, :])
        kv = _dot(xb, w_ref[:, q_off + d_attn:q_off + d_attn + 2 * d_kv]).reshape(nb, tq, 2 * d_kv)
        for bi in range(nb):
            k_b = kv[bi, :, 0:d_kv]
            kext[slot, bi, WINDOW:ext, :] = k_b * cos + _rotate_half(k_b) * sin
            vext[slot, bi, WINDOW:ext, :] = kv[bi, :, d_kv:2 * d_kv]
        nk_ref[...] = kext[slot, :, tq:ext, :]
        nv_ref[...] = vext[slot, :, tq:ext, :]

        low_half_ext = lax.broadcasted_iota(jnp.int32, (ext, LANES), 1) < HEAD_DIM
        for bi in range(nb):
            for src, dst in ((kext, k2), (vext, v2)):
                rows_f = src[slot, bi]
                rows_sw = pltpu.roll(rows_f, HEAD_DIM, 1)
                dst[slot, 0, bi] = jnp.where(low_half_ext, rows_f, rows_sw).astype(BF16)
                dst[slot, 1, bi] = jnp.where(low_half_ext, rows_sw, rows_f).astype(BF16)
        yield

        cos_q = cos * ATTN_SCALE
        sin_q = sin * ATTN_SCALE
        for hf in range(2):
            lo = hf * (d_attn // 2)
            q = _dot(xb, w_ref[:, q_off + lo:q_off + lo + d_attn // 2]).reshape(nb, tq, d_attn // 2)
            for bi in range(nb):
                for p in range(n_pairs // 2):
                    qp = q[bi, :, p * LANES:(p + 1) * LANES]
                    qs[slot, bi, :, lo + p * LANES:lo + (p + 1) * LANES] = (
                        qp * cos_q + _rotate_half(qp) * sin_q).astype(BF16)
            yield

        resid[slot] = alpha * x + _dot(yc[...], wo_ref[0:d_conv, :])

    def stage_b(slot, t):
        low_half = lax.broadcasted_iota(jnp.int32, (CHUNK, LANES), 1) < HEAD_DIM
        low_half_row = lax.broadcasted_iota(jnp.int32, (1, LANES), 1) < HEAD_DIM
        zero_q = jnp.zeros((CHUNK, LANES), BF16)
        for bi in range(nb):
            for c in range(n_chunks):
                r0 = c * CHUNK
                masked = mask_halo and c < WINDOW // CHUNK
                if masked:
                    key_pos = (t * n_chunks + (c - WINDOW // CHUNK)) * CHUNK + lax.broadcasted_iota(
                        jnp.int32, (band, 1), 0)
                    valid = key_pos >= 0
                y_blocks = [None] * n_pairs
                for h in range(N_KV_HEADS):
                    q_stack = []
                    sink_cols = []
                    for gi in range(gqa):
                        n = h * gqa + gi
                        qp = qs[slot, bi, r0:r0 + CHUNK, (n // 2) * LANES:(n // 2 + 1) * LANES]
                        keep = low_half if n % 2 == 0 else jnp.logical_not(low_half)
                        q_stack.append(jnp.where(keep, qp, zero_q))
                        if gi % 2 == 0:
                            sink_cols.append(jnp.where(low_half_row, sink_ref[n], sink_ref[n + 1]))
                    qm = jnp.concatenate(q_stack, axis=0)
                    sink = jnp.concatenate(sink_cols, axis=1)
                    sc = lax.dot_general(k2[slot, h, bi, r0:r0 + band, :], qm, (((1,), (1,)), ((), ())),
                                         preferred_element_type=F32)
                    if masked:
                        sc = jnp.where(valid, sc, MASK_VALUE)
                    m_col = jnp.maximum(jnp.max(sc, axis=0, keepdims=True), sink)
                    e = jnp.exp(sc - m_col)
                    inv_denom = 1.0 / (jnp.sum(e, axis=0, keepdims=True) + jnp.exp(sink - m_col))
                    e = e.astype(BF16)
                    yield
                    o_t = lax.dot_general(v2[slot, h, bi, r0:r0 + band, :], e,
                                          (((0,), (0,)), ((), ())), preferred_element_type=F32)
                    o = (o_t * inv_denom).T
                    for jp in range(gqa // 2):
                        o_even = o[(2 * jp) * CHUNK:(2 * jp + 1) * CHUNK, :]
                        o_odd = o[(2 * jp + 1) * CHUNK:(2 * jp + 2) * CHUNK, :]
                        y_blocks[(h * gqa) // 2 + jp] = jnp.where(low_half, o_even, o_odd)
                y_attn = jnp.concatenate(y_blocks, axis=1)
                ya[bi, r0:r0 + CHUNK, :] = _rms_norm_rows(
                    y_attn, gm_ref[:, d_conv:d_conv + d_attn]).astype(BF16)
        for hf in range(2):
            rh = slice(hf * (rows // 2), (hf + 1) * (rows // 2))
            if nb > 1:
                part, shape = (slice(hf * (nb // 2), (hf + 1) * (nb // 2)), slice(None)), (nb // 2, tq)
            else:
                part, shape = (slice(None), slice(hf * (tq // 2), (hf + 1) * (tq // 2))), (nb, tq // 2)
            y_part = ya[part[0], part[1], :].reshape(rows // 2, d_attn)
            r = resid[slot, rh, :] + _dot(y_part, wo_ref[d_conv:d_conv + d_attn, :])
            o_ref[part[0], part[1], :] = _layer_norm_rows(r, g_ref[...], b_ref[...]).reshape(*shape, -1)

    def run(*stages):
        live = list(stages)
        while live:
            for g in list(live):
                if next(g, StopIteration) is StopIteration:
                    live.remove(g)

    @pl.when(s == 0)
    def _():
        convert()
        run(stage_a(0, 0))

    for parity in range(2):
        @pl.when(jnp.logical_and(jnp.logical_and(s > 0, s < n_tiles), s % 2 == parity))
        def _(parity=parity):
            convert()
            run(stage_b(1 - parity, (s - 1) % n_t), stage_a(parity, s % n_t))

    @pl.when(s == n_tiles)
    def _():
        convert()
        run(stage_b((n_tiles - 1) % 2, (n_tiles - 1) % n_t))


def _mix(x, w_in, w_out, conv_w, sinks, norm_g, ln_g, ln_b, cos, sin, cache_k, cache_v, cache_conv,
         *, nb, tq, mask_halo, alpha, convert=()):
    bsz, t_len, d = x.shape
    d_conv = conv_w.shape[1]
    d_mix = norm_g.shape[1]
    d_attn = d_mix - d_conv
    d_kv = N_KV_HEADS * HEAD_DIM
    assert bsz % nb == 0 and t_len % tq == 0 and tq % CHUNK == 0
    assert d_kv == LANES and d_attn % LANES == 0 and (d_attn // HEAD_DIM) % (2 * N_KV_HEADS) == 0
    assert w_in.shape[1] == 3 * d_conv + d_attn + 2 * d_kv
    n_t = t_len // tq
    assert n_t == 1 or tq >= WINDOW
    n_tiles = (bsz // nb) * n_t
    rows = nb * tq
    ext = WINDOW + tq
    heights = _convert_heights(convert, n_tiles + 1)
    cv_specs, cv_shapes, cv_vmem = _convert_specs(convert, heights, lambda s: s)
    vmem = (w_in.size * BF16_BYTES + w_out.size * BF16_BYTES
            + 2 * 2 * rows * d * F32_BYTES
            + 2 * 2 * tq * LANES * F32_BYTES
            + 2 * 2 * nb * ext * d_kv * F32_BYTES
            + 2 * 2 * N_KV_HEADS * nb * ext * LANES * BF16_BYTES
            + nb * (SUBLANES + tq) * d_conv * F32_BYTES
            + 3 * rows * d_attn * BF16_BYTES
            + rows * d_conv * BF16_BYTES
            + 2 * rows * d * F32_BYTES
            + 4 * 2 * nb * WINDOW * d_kv * F32_BYTES
            + 4 * rows * d_conv * F32_BYTES
            + 2 * rows * d * F32_BYTES
            + cv_vmem)
    kernel = functools.partial(_mix_kernel, nb=nb, tq=tq, n_t=n_t, n_tiles=n_tiles, d_conv=d_conv,
                               d_attn=d_attn, mask_halo=mask_halo, alpha=alpha, n_cv=len(convert))
    tile_a = lambda s: jnp.minimum(s, n_tiles - 1)
    tile_b = lambda s: jnp.maximum(s - 1, 0)
    const = lambda s: (0, 0)
    seq_a = lambda s: (tile_a(s) // n_t, 0, 0)
    return pl.pallas_call(
        kernel,
        grid=(n_tiles + 1,),
        in_specs=[
            pl.BlockSpec(memory_space=pltpu.SMEM),
            pl.BlockSpec((nb, tq, d), lambda s: (tile_a(s) // n_t, tile_a(s) % n_t, 0)),
            pl.BlockSpec(w_in.shape, const, pipeline_mode=pl.Buffered(1)),
            pl.BlockSpec(w_out.shape, const, pipeline_mode=pl.Buffered(1)),
            pl.BlockSpec(conv_w.shape, const),
            pl.BlockSpec(norm_g.shape, const),
            pl.BlockSpec(ln_g.shape, const),
            pl.BlockSpec(ln_b.shape, const),
            pl.BlockSpec((tq, LANES), lambda s: (tile_a(s) % n_t, 0)),
            pl.BlockSpec((tq, LANES), lambda s: (tile_a(s) % n_t, 0)),
            pl.BlockSpec((nb, WINDOW, d_kv), seq_a),
            pl.BlockSpec((nb, WINDOW, d_kv), seq_a),
            pl.BlockSpec((nb, CONV_WIDTH - 1, d_conv), seq_a),
            *cv_specs,
        ],
        out_specs=[
            pl.BlockSpec((nb, tq, d), lambda s: (tile_b(s) // n_t, tile_b(s) % n_t, 0)),
            pl.BlockSpec((nb, CONV_WIDTH - 1, d_conv), seq_a),
            pl.BlockSpec((nb, WINDOW, d_kv), seq_a),
            pl.BlockSpec((nb, WINDOW, d_kv), seq_a),
            *cv_specs,
        ],
        out_shape=[
            jax.ShapeDtypeStruct((bsz, t_len, d), F32),
            jax.ShapeDtypeStruct((bsz, CONV_WIDTH - 1, d_conv), F32),
            jax.ShapeDtypeStruct((bsz, WINDOW, d_kv), F32),
            jax.ShapeDtypeStruct((bsz, WINDOW, d_kv), F32),
            *cv_shapes,
        ],
        scratch_shapes=[
            pltpu.VMEM((2, nb, ext, d_kv), F32),
            pltpu.VMEM((2, nb, ext, d_kv), F32),
            pltpu.VMEM((2, N_KV_HEADS, nb, ext, LANES), BF16),
            pltpu.VMEM((2, N_KV_HEADS, nb, ext, LANES), BF16),
            pltpu.VMEM((nb, SUBLANES + tq, d_conv), F32),
            pltpu.VMEM((2, nb, tq, d_attn), BF16),
            pltpu.VMEM((2, rows, d), F32),
            pltpu.VMEM((nb, tq, d_attn), BF16),
            pltpu.VMEM((rows, d_conv), BF16),
        ],
        compiler_params=pltpu.CompilerParams(
            dimension_semantics=("arbitrary",),
            vmem_limit_bytes=_vmem_limit(vmem)),
        name="mix",
    )(sinks, x, w_in, w_out, conv_w, norm_g, ln_g, ln_b, cos, sin, cache_k, cache_v, cache_conv, *convert)


def _rope_tables(pos):
    half = HEAD_DIM // 2
    inv = ROPE_THETA ** (-jnp.arange(half, dtype=F32) / half)
    ang = pos.astype(F32)[:, None] * inv[None, :]
    cos = jnp.cos(ang)
    sin = jnp.sin(ang)
    reps = LANES // HEAD_DIM
    return jnp.tile(jnp.concatenate([cos, cos], axis=1), (1, reps)), jnp.tile(jnp.concatenate([-sin, sin], axis=1), (1, reps))


def _largest_tile(n, target):
    t = min(n, target)
    while n % t:
        t //= 2
    return t


def _plan(bp, sp, bs, ts):
    ffn_rows = 1024
    ffn_chunk = 512
    ffn_chunk_cast = 256
    mix_rows = 256
    return dict(
        tm_p=_largest_tile(bp * sp, ffn_rows), tm_s=_largest_tile(bs * ts, ffn_rows), tf=ffn_chunk,
        tf_cast=ffn_chunk_cast,
        mix_p=dict(nb=1, tq=_largest_tile(sp, mix_rows)),
        mix_s=dict(nb=_largest_tile(bs, mix_rows // ts), tq=ts))


def kernel(x_prompt, x_sample, cache_conv, cache_k, cache_v, ffn1_w_in, ffn1_w_out, ln1_g, ln1_b, w_mix_in, conv_w, attn_sinks, mix_norm_g, w_mix_out, ln2_g, ln2_b, ffn2_w_in, ffn2_w_out, ln3_g, ln3_b):
    depth = ffn1_w_in.shape[0]
    bp, sp, d = x_prompt.shape
    bs, ts, _ = x_sample.shape
    assert ts == CHUNK
    alpha = (2.0 * depth) ** 0.25
    d_conv = conv_w.shape[-1]
    d_kv = N_KV_HEADS * HEAD_DIM
    plan = _plan(bp, sp, bs, ts)
    tf = _largest_tile(ffn1_w_out.shape[1], plan["tf"])
    tf_cast = _largest_tile(ffn1_w_out.shape[1], plan["tf_cast"])

    cos_p, sin_p = _rope_tables(jnp.arange(sp))
    cos_s, sin_s = _rope_tables(PAST_LEN + jnp.arange(ts))
    zeros_conv = jnp.zeros((bp, CONV_WIDTH - 1, d_conv), F32)
    zeros_kv = jnp.zeros((bp, WINDOW, d_kv), F32)

    xp = x_prompt.reshape(bp * sp, d)
    xs = x_sample.reshape(bs * ts, d)
    outs = {k: [] for k in ("conv_p", "k_p", "v_p", "conv_s", "k_s", "v_s")}
    for l in range(depth):
        g1, b1 = ln1_g[l][None, :], ln1_b[l][None, :]
        g2, b2 = ln2_g[l][None, :], ln2_b[l][None, :]
        g3, b3 = ln3_g[l][None, :], ln3_b[l][None, :]
        gm = mix_norm_g[l][None, :]

        xs, w1_gate, w1_up, w1_out = _ffn(xs, ffn1_w_in[l], ffn1_w_in[l], ffn1_w_out[l], g1, b1, alpha=alpha,
                                          tm=plan["tm_s"], tf=tf_cast, cast_weights=True)
        xp, wm_in, wm_out = _ffn(xp, w1_gate, w1_up, w1_out, g1, b1, alpha=alpha, tm=plan["tm_p"], tf=tf,
                                 convert=(w_mix_in[l], w_mix_out[l]))

        xp, conv_p, k_p, v_p, w2_in, w2_out = _mix(
            xp.reshape(bp, sp, d), wm_in, wm_out, conv_w[l], attn_sinks[l], gm, g2, b2, cos_p, sin_p,
            zeros_kv, zeros_kv, zeros_conv, mask_halo=True, alpha=alpha, **plan["mix_p"],
            convert=(ffn2_w_in[l], ffn2_w_out[l]))
        xs, conv_s, k_s, v_s = _mix(
            xs.reshape(bs, ts, d), wm_in, wm_out, conv_w[l], attn_sinks[l], gm, g2, b2, cos_s, sin_s,
            cache_k[l].reshape(bs, WINDOW, d_kv), cache_v[l].reshape(bs, WINDOW, d_kv), cache_conv[l],
            mask_halo=False, alpha=alpha, **plan["mix_s"])
        xp = xp.reshape(bp * sp, d)
        xs = xs.reshape(bs * ts, d)

        xp, = _ffn(xp, w2_in, w2_in, w2_out, g3, b3, alpha=alpha, tm=plan["tm_p"], tf=tf)
        xs, = _ffn(xs, w2_in, w2_in, w2_out, g3, b3, alpha=alpha, tm=plan["tm_s"], tf=tf)

        kv_shape_p = (bp, WINDOW, N_KV_HEADS, HEAD_DIM)
        kv_shape_s = (bs, WINDOW, N_KV_HEADS, HEAD_DIM)
        outs["conv_p"].append(conv_p)
        outs["k_p"].append(k_p.reshape(kv_shape_p))
        outs["v_p"].append(v_p.reshape(kv_shape_p))
        outs["conv_s"].append(conv_s)
        outs["k_s"].append(k_s.reshape(kv_shape_s))
        outs["v_s"].append(v_s.reshape(kv_shape_s))

    return (xp.reshape(bp, sp, d), xs.reshape(bs, ts, d),
            jnp.stack(outs["conv_p"]), jnp.stack(outs["k_p"]), jnp.stack(outs["v_p"]),
            jnp.stack(outs["conv_s"]), jnp.stack(outs["k_s"]), jnp.stack(outs["v_s"]))
```

```python
import functools

import jax
import jax.numpy as jnp
from jax import lax
from jax.experimental import pallas as pl
from jax.experimental.pallas import tpu as pltpu

CHUNK = 64
HEAD_DIM = 64
N_KV_HEADS = 2
WINDOW = 128
CONV_WIDTH = 3
PAST_LEN = 2048
ROPE_THETA = 10000.0
LN_EPS = 1e-5
RMS_EPS = 1e-6
ATTN_SCALE = HEAD_DIM ** -0.5
MASK_VALUE = -1e30

LANES = 128
SUBLANES = 8
V7X_VMEM_BYTES = 64 * 1024 * 1024
V7X_VMEM_REQUEST_CAP = V7X_VMEM_BYTES - 6 * 1024 * 1024

F32 = jnp.float32
BF16 = jnp.bfloat16
F32_BYTES = 4
BF16_BYTES = 2
BF16_ROW_TILE = 2 * SUBLANES


def _dot(a, b):
    return jnp.dot(a, b, preferred_element_type=F32)


def _layer_norm_rows(y, g, b, scale=1.0):
    mu = jnp.mean(y, axis=-1, keepdims=True)
    d = y - mu
    var = jnp.mean(d * d, axis=-1, keepdims=True)
    return d * lax.rsqrt(var + (scale * scale) * LN_EPS) * g + b


def _rms_norm_rows(y, g):
    inv = lax.rsqrt(jnp.mean(y * y, axis=-1, keepdims=True) + RMS_EPS)
    return y * inv * g


def _vmem_limit(n_bytes):
    assert n_bytes <= V7X_VMEM_REQUEST_CAP, n_bytes
    return int(n_bytes)


def _convert_heights(arrays, n_steps):
    heights = []
    for a in arrays:
        r = a.shape[0]
        heights.append(next(h for h in range(BF16_ROW_TILE, r + 1, BF16_ROW_TILE)
                            if r % h == 0 and r // h <= n_steps))
    return heights


def _convert_specs(arrays, heights, step_of):
    specs, shapes, vmem = [], [], 0
    for a, h in zip(arrays, heights):
        last = a.shape[0] // h - 1
        specs.append(pl.BlockSpec((h, a.shape[1]), lambda *idx, last=last: (jnp.minimum(step_of(*idx), last), 0)))
        shapes.append(jax.ShapeDtypeStruct(a.shape, BF16))
        vmem += 2 * h * a.shape[1] * (F32_BYTES + BF16_BYTES)
    return specs, shapes, vmem


def _convert_blocks(src_refs, dst_refs):
    for src, dst in zip(src_refs, dst_refs):
        dst[...] = src[...].astype(BF16)


def _ffn_kernel(x_hbm, wg_ref, wu_ref, wo_ref, g_ref, b_ref, *rest, alpha, nf, tm, n_cv, cast_weights):
    cv_src, o_ref, cv_dst = rest[:n_cv], rest[n_cv], rest[n_cv + 1:2 * n_cv + 1]
    rest = rest[2 * n_cv + 1:]
    w_outs = w_bufs = (None, None, None)
    if cast_weights:
        w_outs, rest = rest[:3], rest[3:]
    xf_ref, xb_ref, act_ref, x_sem = rest[:4]
    if cast_weights:
        w_bufs = rest[4:]
    i = pl.program_id(0)
    j = pl.program_id(1)
    convert = functools.partial(_convert_blocks, cv_src, cv_dst)

    def x_copy(tile):
        return pltpu.make_async_copy(x_hbm.at[pl.ds(tile * tm, tm), :], xf_ref, x_sem)

    def weight(ref, k):
        if not cast_weights:
            return ref[...]
        w_bufs[k][...] = ref[...].astype(BF16)
        return w_bufs[k][...]

    def gate_up(slot):
        xb = xb_ref[...]
        gate = _dot(xb, weight(wg_ref, 0))
        up = _dot(xb, weight(wu_ref, 1))
        act_ref[slot] = (gate / (1.0 + jnp.exp(-gate)) * up).astype(BF16)

    def down(slot, w, rows=slice(None)):
        o_ref[rows, :] += _dot(act_ref[slot, rows, :], w)

    @pl.when(jnp.logical_and(i == 0, j == 0))
    def _():
        x_copy(0).start()

    @pl.when(j == 0)
    def _():
        convert()
        x_copy(i).wait()
        x = xf_ref[...]
        xb_ref[...] = x.astype(BF16)
        o_ref[...] = (2.0 * alpha) * x
        gate_up(0)

    @pl.when(jnp.logical_and(j == 1, i + 1 < pl.num_programs(0)))
    def _():
        x_copy(i + 1).start()

    for parity in range(2):
        @pl.when(jnp.logical_and(jnp.logical_and(j > 0, j < nf), j % 2 == parity))
        def _(parity=parity):
            convert()
            gate_up(parity)
            down(1 - parity, weight(wo_ref, 2))

    @pl.when(j == nf)
    def _():
        convert()
        w = weight(wo_ref, 2)
        for r in range(4):
            rows = slice(r * (tm // 4), (r + 1) * (tm // 4))
            down((nf - 1) % 2, w, rows)
            o_ref[rows, :] = _layer_norm_rows(o_ref[rows, :], g_ref[...], b_ref[...], scale=2.0)

    if cast_weights:
        @pl.when(i == 0)
        def _():
            w_outs[0][...] = w_bufs[0][...]
            w_outs[1][...] = w_bufs[1][...]

        @pl.when(jnp.logical_and(i == 0, j > 0))
        def _():
            w_outs[2][...] = w_bufs[2][...]


def _ffn(x, w_gate, w_up, w_out, g, b, *, alpha, tm, tf, convert=(), cast_weights=False):
    m, d = x.shape
    f = w_out.shape[0]
    assert m % tm == 0 and f % tf == 0 and tm % (2 * SUBLANES) == 0
    nf = f // tf
    up_offset = nf if w_up.shape[1] == 2 * f else 0
    heights = _convert_heights(convert, (m // tm) * (nf + 1))
    cv_specs, cv_shapes, cv_vmem = _convert_specs(convert, heights, lambda i, j: i * (nf + 1) + j)
    w_bytes = F32_BYTES + BF16_BYTES if cast_weights else BF16_BYTES
    vmem = ((3 * d * tf * BF16_BYTES if cast_weights else 0)
            + tm * d * F32_BYTES
            + 2 * tm * d * F32_BYTES
            + tm * d * BF16_BYTES
            + 2 * tm * tf * BF16_BYTES
            + 2 * 3 * d * tf * w_bytes
            + 2 * tm * tf * F32_BYTES
            + (tm // 2) * d * F32_BYTES
            + cv_vmem)
    gate_spec = pl.BlockSpec((d, tf), lambda i, j: (0, jnp.minimum(j, nf - 1)))
    up_spec = pl.BlockSpec((d, tf), lambda i, j: (0, jnp.minimum(j, nf - 1) + up_offset))
    down_spec = pl.BlockSpec((tf, d), lambda i, j: (jnp.maximum(j - 1, 0), 0))
    w_specs = [gate_spec, up_spec, down_spec]
    w_out_specs, w_out_shapes, w_scratch = [], [], []
    if cast_weights:
        in_out = pl.BlockSpec((d, tf), lambda i, j: (0, jnp.where(i == 0, jnp.minimum(j, nf - 1), nf - 1)))
        down_out = pl.BlockSpec((tf, d), lambda i, j: (jnp.where(i == 0, jnp.maximum(j - 1, 0), nf - 1), 0))
        w_out_specs = [in_out, in_out, down_out]
        w_out_shapes = [jax.ShapeDtypeStruct((d, f), BF16)] * 2 + [jax.ShapeDtypeStruct(w_out.shape, BF16)]
        w_scratch = [pltpu.VMEM((d, tf), BF16), pltpu.VMEM((d, tf), BF16), pltpu.VMEM((tf, d), BF16)]
    return pl.pallas_call(
        functools.partial(_ffn_kernel, alpha=alpha, nf=nf, tm=tm, n_cv=len(convert), cast_weights=cast_weights),
        grid=(m // tm, nf + 1),
        in_specs=[
            pl.BlockSpec(memory_space=pl.ANY),
            *w_specs,
            pl.BlockSpec((1, d), lambda i, j: (0, 0)),
            pl.BlockSpec((1, d), lambda i, j: (0, 0)),
            *cv_specs,
        ],
        out_specs=[pl.BlockSpec((tm, d), lambda i, j: (i, 0)), *cv_specs, *w_out_specs],
        out_shape=[jax.ShapeDtypeStruct((m, d), F32), *cv_shapes, *w_out_shapes],
        scratch_shapes=[pltpu.VMEM((tm, d), F32), pltpu.VMEM((tm, d), BF16), pltpu.VMEM((2, tm, tf), BF16),
                        pltpu.SemaphoreType.DMA, *w_scratch],
        compiler_params=pltpu.CompilerParams(
            dimension_semantics=("arbitrary", "arbitrary"),
            vmem_limit_bytes=_vmem_limit(vmem)),
        name="ffn",
    )(x, w_gate, w_up, w_out, g, b, *convert)


def _rotate_half(x):
    lane = lax.broadcasted_iota(jnp.int32, x.shape, 1)
    first_half = (lane % HEAD_DIM) < (HEAD_DIM // 2)
    return jnp.where(first_half, pltpu.roll(x, LANES - HEAD_DIM // 2, 1), pltpu.roll(x, HEAD_DIM // 2, 1))


def _mix_kernel(sink_ref, x_ref, w_ref, wo_ref, cw_ref, gm_ref, g_ref, b_ref, cos_ref, sin_ref,
                ck_ref, cv_ref, cc_ref, *rest,
                nb, tq, n_t, n_tiles, d_conv, d_attn, mask_halo, alpha, n_cv):
    cv_src, (o_ref, nconv_ref, nk_ref, nv_ref) = rest[:n_cv], rest[n_cv:n_cv + 4]
    cv_dst = rest[n_cv + 4:2 * n_cv + 4]
    kext, vext, k2, v2, uext, qs, resid, ya, yc = rest[2 * n_cv + 4:]
    s = pl.program_id(0)
    convert = functools.partial(_convert_blocks, cv_src, cv_dst)
    d_kv = N_KV_HEADS * HEAD_DIM
    gqa = d_attn // HEAD_DIM // N_KV_HEADS
    pad = SUBLANES
    tail = CONV_WIDTH - 1
    rows = nb * tq
    band = WINDOW + CHUNK
    ext = WINDOW + tq
    n_chunks = tq // CHUNK
    n_pairs = d_attn // LANES
    q_off = 3 * d_conv

    def stage_a(slot, t):
        first = t == 0
        x = x_ref[...].reshape(rows, x_ref.shape[-1])
        xb = x.astype(BF16)
        half = d_conv // 2
        col = lambda g, hf: slice(g * d_conv + hf * half, g * d_conv + (hf + 1) * half)

        b_gate = []
        for hf in range(2):
            b_gate.append(_dot(xb, w_ref[:, col(0, hf)]))
            yield
        c_gate = []
        for hf in range(2):
            c_gate.append(_dot(xb, w_ref[:, col(1, hf)]))
            yield
        y_half = []
        for hf in range(2):
            ch = slice(hf * half, (hf + 1) * half)
            hc = _dot(xb, w_ref[:, col(2, hf)])
            u = (c_gate[hf] * hc).reshape(nb, tq, half)
            uext[:, pad - tail:pad, ch] = jnp.where(first, cc_ref[:, :, ch], uext[:, pad + tq - tail:pad + tq, ch])
            uext[:, pad:pad + tq, ch] = u
            z = cw_ref[CONV_WIDTH - 1:CONV_WIDTH, ch] * u
            for jj in range(tail):
                off = pad - tail + jj
                z = z + cw_ref[jj:jj + 1, ch] * uext[:, off:off + tq, ch]
            y_half.append(b_gate[hf].reshape(nb, tq, half) * z)
            if hf == 0:
                yield
        nconv_ref[...] = uext[:, pad + tq - tail:pad + tq, :]
        ssq = sum(jnp.sum(y * y, axis=-1, keepdims=True) for y in y_half)
        inv = lax.rsqrt(ssq * (1.0 / d_conv) + RMS_EPS)
        for hf in range(2):
            ch = slice(hf * half, (hf + 1) * half)
            yc[:, ch] = (y_half[hf] * inv * gm_ref[:, ch]).reshape(rows, half).astype(BF16)
        yield

        cos = cos_ref[...]
        sin = sin_ref[...]
        kext[slot, :, 0:WINDOW, :] = jnp.where(first, ck_ref[...], kext[1 - slot, :, tq:ext, :])
        vext[slot, :, 0:WINDOW, :] = jnp.where(first, cv_ref[...], vext[1 - slot, :, tq:ext, :])
        kv = _dot(xb, w_ref[:, q_off + d_attn:q_off + d_attn + 2 * d_kv]).reshape(nb, tq, 2 * d_kv)
        for bi in range(nb):
            k_b = kv[bi, :, 0:d_kv]
            kext[slot, bi, WINDOW:ext, :] = k_b * cos + _rotate_half(k_b) * sin
            vext[slot, bi, WINDOW:ext, :] = kv[bi, :, d_kv:2 * d_kv]
        nk_ref[...] = kext[slot, :, tq:ext, :]
        nv_ref[...] = vext[slot, :, tq:ext, :]

        low_half_ext = lax.broadcasted_iota(jnp.int32, (ext, LANES), 1) < HEAD_DIM
        for bi in range(nb):
            for src, dst in ((kext, k2), (vext, v2)):
                rows_f = src[slot, bi]
                rows_sw = pltpu.roll(rows_f, HEAD_DIM, 1)
                dst[slot, 0, bi] = jnp.where(low_half_ext, rows_f, rows_sw).astype(BF16)
                dst[slot, 1, bi] = jnp.where(low_half_ext, rows_sw, rows_f).astype(BF16)
        yield

        cos_q = cos * ATTN_SCALE
        sin_q = sin * ATTN_SCALE
        for hf in range(2):
            lo = hf * (d_attn // 2)
            q = _dot(xb, w_ref[:, q_off + lo:q_off + lo + d_attn // 2]).reshape(nb, tq, d_attn // 2)
            for bi in range(nb):
                for p in range(n_pairs // 2):
                    qp = q[bi, :, p * LANES:(p + 1) * LANES]
                    qs[slot, bi, :, lo + p * LANES:lo + (p + 1) * LANES] = (
                        qp * cos_q + _rotate_half(qp) * sin_q).astype(BF16)
            yield

        resid[slot] = alpha * x + _dot(yc[...], wo_ref[0:d_conv, :])

    def stage_b(slot, t):
        low_half = lax.broadcasted_iota(jnp.int32, (CHUNK, LANES), 1) < HEAD_DIM
        low_half_row = lax.broadcasted_iota(jnp.int32, (1, LANES), 1) < HEAD_DIM
        zero_q = jnp.zeros((CHUNK, LANES), BF16)
        for bi in range(nb):
            for c in range(n_chunks):
                r0 = c * CHUNK
                masked = mask_halo and c < WINDOW // CHUNK
                if masked:
                    key_pos = (t * n_chunks + (c - WINDOW // CHUNK)) * CHUNK + lax.broadcasted_iota(
                        jnp.int32, (band, 1), 0)
                    valid = key_pos >= 0
                y_blocks = [None] * n_pairs
                for h in range(N_KV_HEADS):
                    q_stack = []
                    sink_cols = []
                    for gi in range(gqa):
                        n = h * gqa + gi
                        qp = qs[slot, bi, r0:r0 + CHUNK, (n // 2) * LANES:(n // 2 + 1) * LANES]
                        keep = low_half if n % 2 == 0 else jnp.logical_not(low_half)
                        q_stack.append(jnp.where(keep, qp, zero_q))
                        if gi % 2 == 0:
                            sink_cols.append(jnp.where(low_half_row, sink_ref[n], sink_ref[n + 1]))
                    qm = jnp.concatenate(q_stack, axis=0)
                    sink = jnp.concatenate(sink_cols, axis=1)
                    sc = lax.dot_general(k2[slot, h, bi, r0:r0 + band, :], qm, (((1,), (1,)), ((), ())),
                                         preferred_element_type=F32)
                    if masked:
                        sc = jnp.where(valid, sc, MASK_VALUE)
                    m_col = jnp.maximum(jnp.max(sc, axis=0, keepdims=True), sink)
                    e = jnp.exp(sc - m_col)
                    inv_denom = 1.0 / (jnp.sum(e, axis=0, keepdims=True) + jnp.exp(sink - m_col))
                    e = e.astype(BF16)
                    yield
                    o_t = lax.dot_general(v2[slot, h, bi, r0:r0 + band, :], e,
                                          (((0,), (0,)), ((), ())), preferred_element_type=F32)
                    o = (o_t * inv_denom).T
                    for jp in range(gqa // 2):
                        o_even = o[(2 * jp) * CHUNK:(2 * jp + 1) * CHUNK, :]
                        o_odd = o[(2 * jp + 1) * CHUNK:(2 * jp + 2) * CHUNK, :]
                        y_blocks[(h * gqa) // 2 + jp] = jnp.where(low_half, o_even, o_odd)
                y_attn = jnp.concatenate(y_blocks, axis=1)
                ya[bi, r0:r0 + CHUNK, :] = _rms_norm_rows(
                    y_attn, gm_ref[:, d_conv:d_conv + d_attn]).astype(BF16)
        n_split = 2 if nb == 1 else 1
        for hf in range(n_split):
            t_part = slice(hf * (tq // n_split), (hf + 1) * (tq // n_split))
            r_part = slice(hf * (rows // n_split), (hf + 1) * (rows // n_split))
            y_part = ya[:, t_part, :].reshape(rows // n_split, d_attn)
            r = resid[slot, r_part, :] + _dot(y_part, wo_ref[d_conv:d_conv + d_attn, :])
            o_ref[:, t_part, :] = _layer_norm_rows(r, g_ref[...], b_ref[...]).reshape(nb, tq // n_split, -1)

    def run(*stages):
        live = list(stages)
        while live:
            for g in list(live):
                if next(g, StopIteration) is StopIteration:
                    live.remove(g)

    @pl.when(s == 0)
    def _():
        convert()
        run(stage_a(0, 0))

    for parity in range(2):
        @pl.when(jnp.logical_and(jnp.logical_and(s > 0, s < n_tiles), s % 2 == parity))
        def _(parity=parity):
            convert()
            run(stage_b(1 - parity, (s - 1) % n_t), stage_a(parity, s % n_t))

    @pl.when(s == n_tiles)
    def _():
        convert()
        run(stage_b((n_tiles - 1) % 2, (n_tiles - 1) % n_t))


def _mix(x, w_in, w_out, conv_w, sinks, norm_g, ln_g, ln_b, cos, sin, cache_k, cache_v, cache_conv,
         *, nb, tq, mask_halo, alpha, convert=()):
    bsz, t_len, d = x.shape
    d_conv = conv_w.shape[1]
    d_mix = norm_g.shape[1]
    d_attn = d_mix - d_conv
    d_kv = N_KV_HEADS * HEAD_DIM
    assert bsz % nb == 0 and t_len % tq == 0 and tq % CHUNK == 0
    assert d_kv == LANES and d_attn % LANES == 0 and (d_attn // HEAD_DIM) % (2 * N_KV_HEADS) == 0
    assert w_in.shape[1] == 3 * d_conv + d_attn + 2 * d_kv
    n_t = t_len // tq
    assert n_t == 1 or tq >= WINDOW
    n_tiles = (bsz // nb) * n_t
    rows = nb * tq
    ext = WINDOW + tq
    heights = _convert_heights(convert, n_tiles + 1)
    cv_specs, cv_shapes, cv_vmem = _convert_specs(convert, heights, lambda s: s)
    vmem = (w_in.size * BF16_BYTES + w_out.size * BF16_BYTES
            + 2 * 2 * rows * d * F32_BYTES
            + 2 * 2 * tq * LANES * F32_BYTES
            + 2 * 2 * nb * ext * d_kv * F32_BYTES
            + 2 * 2 * N_KV_HEADS * nb * ext * LANES * BF16_BYTES
            + nb * (SUBLANES + tq) * d_conv * F32_BYTES
            + 3 * rows * d_attn * BF16_BYTES
            + rows * d_conv * BF16_BYTES
            + 2 * rows * d * F32_BYTES
            + 4 * 2 * nb * WINDOW * d_kv * F32_BYTES
            + 4 * rows * d_conv * F32_BYTES
            + 2 * rows * d * F32_BYTES
            + cv_vmem)
    kernel = functools.partial(_mix_kernel, nb=nb, tq=tq, n_t=n_t, n_tiles=n_tiles, d_conv=d_conv,
                               d_attn=d_attn, mask_halo=mask_halo, alpha=alpha, n_cv=len(convert))
    tile_a = lambda s: jnp.minimum(s, n_tiles - 1)
    tile_b = lambda s: jnp.maximum(s - 1, 0)
    const = lambda s: (0, 0)
    seq_a = lambda s: (tile_a(s) // n_t, 0, 0)
    return pl.pallas_call(
        kernel,
        grid=(n_tiles + 1,),
        in_specs=[
            pl.BlockSpec(memory_space=pltpu.SMEM),
            pl.BlockSpec((nb, tq, d), lambda s: (tile_a(s) // n_t, tile_a(s) % n_t, 0)),
            pl.BlockSpec(w_in.shape, const, pipeline_mode=pl.Buffered(1)),
            pl.BlockSpec(w_out.shape, const, pipeline_mode=pl.Buffered(1)),
            pl.BlockSpec(conv_w.shape, const),
            pl.BlockSpec(norm_g.shape, const),
            pl.BlockSpec(ln_g.shape, const),
            pl.BlockSpec(ln_b.shape, const),
            pl.BlockSpec((tq, LANES), lambda s: (tile_a(s) % n_t, 0)),
            pl.BlockSpec((tq, LANES), lambda s: (tile_a(s) % n_t, 0)),
            pl.BlockSpec((nb, WINDOW, d_kv), seq_a),
            pl.BlockSpec((nb, WINDOW, d_kv), seq_a),
            pl.BlockSpec((nb, CONV_WIDTH - 1, d_conv), seq_a),
            *cv_specs,
        ],
        out_specs=[
            pl.BlockSpec((nb, tq, d), lambda s: (tile_b(s) // n_t, tile_b(s) % n_t, 0)),
            pl.BlockSpec((nb, CONV_WIDTH - 1, d_conv), seq_a),
            pl.BlockSpec((nb, WINDOW, d_kv), seq_a),
            pl.BlockSpec((nb, WINDOW, d_kv), seq_a),
            *cv_specs,
        ],
        out_shape=[
            jax.ShapeDtypeStruct((bsz, t_len, d), F32),
            jax.ShapeDtypeStruct((bsz, CONV_WIDTH - 1, d_conv), F32),
            jax.ShapeDtypeStruct((bsz, WINDOW, d_kv), F32),
            jax.ShapeDtypeStruct((bsz, WINDOW, d_kv), F32),
            *cv_shapes,
        ],
        scratch_shapes=[
            pltpu.VMEM((2, nb, ext, d_kv), F32),
            pltpu.VMEM((2, nb, ext, d_kv), F32),
            pltpu.VMEM((2, N_KV_HEADS, nb, ext, LANES), BF16),
            pltpu.VMEM((2, N_KV_HEADS, nb, ext, LANES), BF16),
            pltpu.VMEM((nb, SUBLANES + tq, d_conv), F32),
            pltpu.VMEM((2, nb, tq, d_attn), BF16),
            pltpu.VMEM((2, rows, d), F32),
            pltpu.VMEM((nb, tq, d_attn), BF16),
            pltpu.VMEM((rows, d_conv), BF16),
        ],
        compiler_params=pltpu.CompilerParams(
            dimension_semantics=("arbitrary",),
            vmem_limit_bytes=_vmem_limit(vmem)),
        name="mix",
    )(sinks, x, w_in, w_out, conv_w, norm_g, ln_g, ln_b, cos, sin, cache_k, cache_v, cache_conv, *convert)


def _rope_tables(pos):
    half = HEAD_DIM // 2
    inv = ROPE_THETA ** (-jnp.arange(half, dtype=F32) / half)
    ang = pos.astype(F32)[:, None] * inv[None, :]
    cos = jnp.cos(ang)
    sin = jnp.sin(ang)
    reps = LANES // HEAD_DIM
    return jnp.tile(jnp.concatenate([cos, cos], axis=1), (1, reps)), jnp.tile(jnp.concatenate([-sin, sin], axis=1), (1, reps))


def _largest_tile(n, target):
    t = min(n, target)
    while n % t:
        t //= 2
    return t


def _plan(bp, sp, bs, ts):
    ffn_rows = 1024
    ffn_chunk = 512
    ffn_chunk_cast = 256
    mix_rows = 256
    return dict(
        tm_p=_largest_tile(bp * sp, ffn_rows), tm_s=_largest_tile(bs * ts, ffn_rows), tf=ffn_chunk,
        tf_cast=ffn_chunk_cast,
        mix_p=dict(nb=1, tq=_largest_tile(sp, mix_rows)),
        mix_s=dict(nb=_largest_tile(bs, mix_rows // ts), tq=ts))


def kernel(x_prompt, x_sample, cache_conv, cache_k, cache_v, ffn1_w_in, ffn1_w_out, ln1_g, ln1_b, w_mix_in, conv_w, attn_sinks, mix_norm_g, w_mix_out, ln2_g, ln2_b, ffn2_w_in, ffn2_w_out, ln3_g, ln3_b):
    depth = ffn1_w_in.shape[0]
    bp, sp, d = x_prompt.shape
    bs, ts, _ = x_sample.shape
    assert ts == CHUNK
    alpha = (2.0 * depth) ** 0.25
    d_conv = conv_w.shape[-1]
    d_kv = N_KV_HEADS * HEAD_DIM
    plan = _plan(bp, sp, bs, ts)
    tf = _largest_tile(ffn1_w_out.shape[1], plan["tf"])
    tf_cast = _largest_tile(ffn1_w_out.shape[1], plan["tf_cast"])

    cos_p, sin_p = _rope_tables(jnp.arange(sp))
    cos_s, sin_s = _rope_tables(PAST_LEN + jnp.arange(ts))
    zeros_conv = jnp.zeros((bp, CONV_WIDTH - 1, d_conv), F32)
    zeros_kv = jnp.zeros((bp, WINDOW, d_kv), F32)

    xp = x_prompt.reshape(bp * sp, d)
    xs = x_sample.reshape(bs * ts, d)
    outs = {k: [] for k in ("conv_p", "k_p", "v_p", "conv_s", "k_s", "v_s")}
    for l in range(depth):
        g1, b1 = ln1_g[l][None, :], ln1_b[l][None, :]
        g2, b2 = ln2_g[l][None, :], ln2_b[l][None, :]
        g3, b3 = ln3_g[l][None, :], ln3_b[l][None, :]
        gm = mix_norm_g[l][None, :]

        xs, w1_gate, w1_up, w1_out = _ffn(xs, ffn1_w_in[l], ffn1_w_in[l], ffn1_w_out[l], g1, b1, alpha=alpha,
                                          tm=plan["tm_s"], tf=tf_cast, cast_weights=True)
        xp, wm_in, wm_out = _ffn(xp, w1_gate, w1_up, w1_out, g1, b1, alpha=alpha, tm=plan["tm_p"], tf=tf,
                                 convert=(w_mix_in[l], w_mix_out[l]))

        xp, conv_p, k_p, v_p, w2_in, w2_out = _mix(
            xp.reshape(bp, sp, d), wm_in, wm_out, conv_w[l], attn_sinks[l], gm, g2, b2, cos_p, sin_p,
            zeros_kv, zeros_kv, zeros_conv, mask_halo=True, alpha=alpha, **plan["mix_p"],
            convert=(ffn2_w_in[l], ffn2_w_out[l]))
        xs, conv_s, k_s, v_s = _mix(
            xs.reshape(bs, ts, d), wm_in, wm_out, conv_w[l], attn_sinks[l], gm, g2, b2, cos_s, sin_s,
            cache_k[l].reshape(bs, WINDOW, d_kv), cache_v[l].reshape(bs, WINDOW, d_kv), cache_conv[l],
            mask_halo=False, alpha=alpha, **plan["mix_s"])
        xp = xp.reshape(bp * sp, d)
        xs = xs.reshape(bs * ts, d)

        xp, = _ffn(xp, w2_in, w2_in, w2_out, g3, b3, alpha=alpha, tm=plan["tm_p"], tf=tf)
        xs, = _ffn(xs, w2_in, w2_in, w2_out, g3, b3, alpha=alpha, tm=plan["tm_s"], tf=tf)

        kv_shape_p = (bp, WINDOW, N_KV_HEADS, HEAD_DIM)
        kv_shape_s = (bs, WINDOW, N_KV_HEADS, HEAD_DIM)
        outs["conv_p"].append(conv_p)
        outs["k_p"].append(k_p.reshape(kv_shape_p))
        outs["v_p"].append(v_p.reshape(kv_shape_p))
        outs["conv_s"].append(conv_s)
        outs["k_s"].append(k_s.reshape(kv_shape_s))
        outs["v_s"].append(v_s.reshape(kv_shape_s))

    return (xp.reshape(bp, sp, d), xs.reshape(bs, ts, d),
            jnp.stack(outs["conv_p"]), jnp.stack(outs["k_p"]), jnp.stack(outs["v_p"]),
            jnp.stack(outs["conv_s"]), jnp.stack(outs["k_s"]), jnp.stack(outs["v_s"]))
```
